```python
import math
import jax, jax.numpy as jnp
from jax import lax
import numpy as np

D_MODEL = 1024
BATCH = 8
SEQ = 4096
DEPTH = 4
DEC_BATCH = 2
DEC_SEQ = 8192
PAST_LEN = 128

N_META = 16
N_MIXERS = 4
DN_ALPHA = (2 * DEPTH) ** 0.25
DN_BETA = (8 * DEPTH) ** -0.25
LN_EPS = 1e-5
RMS_EPS = 1e-6
ROPE_THETA = 10000.0

HG_HEADS = 8
HG_FDIM = 128
HG_IDIM = D_MODEL // HG_HEADS
HG_CHUNK = 16
POOL_WINDOWS = (2, 4, 8, 16)
POOL_GROUP = D_MODEL // len(POOL_WINDOWS)
SW_HEADS = 8
SW_KV_HEADS = 2
SW_GROUPS = SW_HEADS // SW_KV_HEADS
SW_HEAD_DIM = D_MODEL // SW_HEADS
SW_WINDOW = 128
SW_BLOCK = 128
MLA_HEADS = 16
MLA_Q_RANK = 256
MLA_KV_RANK = 256
MLA_NOPE = 128
MLA_ROPE = 64
MLA_V = 128
MLA_BLOCK = 128
D_FF = ((8 * D_MODEL + 3 * 256 - 1) // (3 * 256)) * 256

N_A = (DEPTH + 3) // 4
N_B = (DEPTH + 2) // 4
N_C = (DEPTH + 1) // 4
N_D = DEPTH // 4

kernel_name = "hybrid_bidir_hgrn2_pool_swa_mla_encoder"


def layer_norm(x, g, b):
    xf = x.astype(jnp.float32)
    mu = jnp.mean(xf, axis=-1, keepdims=True)
    xc = xf - mu
    var = jnp.mean(xc * xc, axis=-1, keepdims=True)
    return (xc * lax.rsqrt(var + LN_EPS) * g.astype(jnp.float32) + b.astype(jnp.float32)).astype(x.dtype)


def rms_norm(x, g):
    xf = x.astype(jnp.float32)
    return (xf * lax.rsqrt(jnp.mean(xf * xf, axis=-1, keepdims=True) + RMS_EPS) * g.astype(jnp.float32)).astype(x.dtype)


def rope(x, pos):
    d = x.shape[-1]
    inv = 1.0 / (ROPE_THETA ** (jnp.arange(0, d, 2, dtype=jnp.float32) / d))
    ang = pos.astype(jnp.float32)[:, None] * inv[None, :]
    cos = jnp.cos(ang)[:, None, :]
    sin = jnp.sin(ang)[:, None, :]
    xf = x.astype(jnp.float32)
    x1, x2 = xf[..., : d // 2], xf[..., d // 2:]
    return jnp.concatenate([x1 * cos - x2 * sin, x2 * cos + x1 * sin], axis=-1).astype(x.dtype)


def pad_time(a, front, back):
    return jnp.pad(a, [(0, 0), (front, back)] + [(0, 0)] * (a.ndim - 2))


def hgrn2_scan(q, k, v, logf):
    B, T, H, F = q.shape
    E = v.shape[-1]
    n = T // HG_CHUNK

    def chunks(a):
        return jnp.moveaxis(a.reshape(B, n, HG_CHUNK, H, a.shape[-1]), 1, 0)

    causal = jnp.tril(jnp.ones((HG_CHUNK, HG_CHUNK), dtype=bool))

    def step(S, inp):
        qc, kc, vc, lf = inp
        b = jnp.cumsum(lf, axis=1)
        bl = b[:, -1:]
        qe = qc * jnp.exp(b)
        ke = kc * jnp.exp(-b)
        A = jnp.where(causal, jnp.einsum('bthf,bshf->bhts', qe, ke), 0.0)
        o = jnp.einsum('bhts,bshe->bthe', A, vc) + jnp.einsum('bthf,bhfe->bthe', qe, S)
        S = jnp.exp(bl[:, 0])[..., None] * S + jnp.einsum('bshf,bshe->bhfe', kc * jnp.exp(bl - b), vc)
        return S, o

    S0 = jnp.zeros((B, H, F, E), jnp.float32)
    _, o = lax.scan(step, S0, (chunks(q), chunks(k), chunks(v), chunks(logf)))
    return jnp.moveaxis(o, 0, 1).reshape(B, T, H, E)


def hgrn2_mixer(x, w_in, w_out, norm_g, lb):
    B, T, _ = x.shape
    HF = HG_HEADS * HG_FDIM
    HE = HG_HEADS * HG_IDIM
    proj = x @ w_in
    q, i, ff, fb, g = jnp.split(proj, [HF, HF + HE, 2 * HF + HE, 3 * HF + HE], axis=-1)
    heads = lambda a: a.reshape(B, T, HG_HEADS, -1).astype(jnp.float32)
    q = jax.nn.silu(heads(q))
    v = heads(i)
    lbh = lb.astype(jnp.float32).reshape(HG_HEADS, HG_FDIM)

    def gates(a):
        a = heads(a)
        f = lbh + (1.0 - lbh) * jax.nn.sigmoid(a)
        return (1.0 - lbh) * jax.nn.sigmoid(-a), jnp.log(f)

    k_f, lf_f = gates(ff)
    k_b, lf_b = gates(fb)
    o_fwd = hgrn2_scan(q, k_f, v, lf_f)
    flip = lambda a: jnp.flip(a, axis=1)
    o_bwd = flip(hgrn2_scan(flip(q), flip(k_b), flip(v), flip(lf_b)))
    o = rms_norm(o_fwd + o_bwd, norm_g) * jax.nn.silu(heads(g))
    return o.reshape(B, T, HE).astype(x.dtype) @ w_out


def pool_mixer(x, w_grp, scale):
    B, T, D = x.shape
    xf = x.astype(jnp.float32)
    cs = jnp.concatenate([jnp.zeros((B, 1, D), jnp.float32), jnp.cumsum(xf, axis=1)], axis=1)
    t = jnp.arange(T)
    outs = []
    for gi, w in enumerate(POOL_WINDOWS):
        sl = slice(gi * POOL_GROUP, (gi + 1) * POOL_GROUP)
        lo = jnp.clip(t - w // 2, 0, T)
        hi = jnp.clip(t + w // 2, 0, T)
        csg = cs[:, :, sl]
        mean = (csg[:, hi] - csg[:, lo]) / (hi - lo).astype(jnp.float32)[:, None]
        outs.append(mean - xf[:, :, sl])
    p = jnp.stack(outs, axis=2).astype(x.dtype)
    y = jnp.einsum('btgc,gcd->btgd', p, w_grp).reshape(B, T, D)
    return y * scale


def swa_mixer(x, w_qkv, w_out, sink):
    B, T, _ = x.shape
    HQ, HK, G, HD, BLK = SW_HEADS, SW_KV_HEADS, SW_GROUPS, SW_HEAD_DIM, SW_BLOCK
    pos = jnp.arange(T)
    q, k, v = jnp.split(x @ w_qkv, [HQ * HD, (HQ + HK) * HD], axis=-1)
    q = rope(q.reshape(B, T, HQ, HD), pos).reshape(B, T, HK, G, HD)
    k = rope(k.reshape(B, T, HK, HD), pos)
    v = v.reshape(B, T, HK, HD)
    pad = (-T) % BLK
    Tp = T + pad
    nb = Tp // BLK
    qb = pad_time(q, pad, 0).reshape(B, nb, BLK, HK, G, HD)
    kb = pad_time(k, pad + BLK, BLK).reshape(B, nb + 2, BLK, HK, HD)
    vb = pad_time(v, pad + BLK, BLK).reshape(B, nb + 2, BLK, HK, HD)
    kw = jnp.concatenate([kb[:, :-2], kb[:, 1:-1], kb[:, 2:]], axis=2)
    vw = jnp.concatenate([vb[:, :-2], vb[:, 1:-1], vb[:, 2:]], axis=2)
    qpos = jnp.arange(Tp).reshape(nb, BLK) - pad
    kpos = jnp.arange(nb)[:, None] * BLK + jnp.arange(3 * BLK)[None, :] - BLK - pad
    valid = ((kpos[:, None, :] >= 0) & (kpos[:, None, :] < T)
             & (jnp.abs(qpos[:, :, None] - kpos[:, None, :]) <= SW_WINDOW))
    s = jnp.einsum('bnqkgd,bnskd->bnkgqs', qb, kw).astype(jnp.float32) * (HD ** -0.5)
    s = jnp.where(valid[None, :, None, None], s, -jnp.inf)
    sk = sink.astype(jnp.float32).reshape(HK, G)[None, None, :, :, None, None]
    m = jnp.maximum(jnp.max(s, axis=-1, keepdims=True), sk)
    p = jnp.exp(s - m)
    den = jnp.sum(p, axis=-1, keepdims=True) + jnp.exp(sk - m)
    o = jnp.einsum('bnkgqs,bnskd->bnqkgd', (p / den).astype(x.dtype), vw)
    o = o.reshape(B, Tp, HQ * HD)[:, pad:]
    return o @ w_out


def mla_mixer(x, w_dq, q_norm_g, w_uq, w_dkv, kv_norm_g, w_ukv, w_out):
    B, T, _ = x.shape
    H, BLK = MLA_HEADS, MLA_BLOCK
    pos = jnp.arange(T)
    cq = rms_norm(x @ w_dq, q_norm_g)
    q = (cq @ w_uq).reshape(B, T, H, MLA_NOPE + MLA_ROPE)
    q_nope, q_rope = q[..., :MLA_NOPE], rope(q[..., MLA_NOPE:], pos)
    ckv = x @ w_dkv
    c = rms_norm(ckv[..., :MLA_KV_RANK], kv_norm_g)
    k_rope = rope(ckv[..., MLA_KV_RANK:][:, :, None, :], pos)[:, :, 0]
    kv = (c @ w_ukv).reshape(B, T, H, MLA_NOPE + MLA_V)
    k_nope, v = kv[..., :MLA_NOPE], kv[..., MLA_NOPE:]
    pad = (-T) % BLK
    nb = (T + pad) // BLK
    to_blocks = lambda a: jnp.moveaxis(pad_time(a, pad, 0).reshape(B, nb, BLK, H, a.shape[-1]), 1, 0)
    scale = (MLA_NOPE + MLA_ROPE) ** -0.5

    def block(args):
        qn, qr = args
        s = (jnp.einsum('bqhd,bkhd->bhqk', qn, k_nope)
             + jnp.einsum('bqhd,bkd->bhqk', qr, k_rope)).astype(jnp.float32) * scale
        p = jax.nn.softmax(s, axis=-1)
        return jnp.einsum('bhqk,bkhd->bqhd', p.astype(x.dtype), v)

    o = lax.map(block, (to_blocks(q_nope), to_blocks(q_rope)))
    o = jnp.moveaxis(o, 0, 1).reshape(B, nb * BLK, H * MLA_V)[:, pad:]
    return o @ w_out


def swiglu(x, w_gu, w_down):
    g, u = jnp.split(x @ w_gu, 2, axis=-1)
    return (jax.nn.silu(g) * u) @ w_down


def trunk(x, p):
    B = x.shape[0]
    meta = jnp.broadcast_to(p['meta_tokens'].astype(x.dtype)[None], (B, N_META, D_MODEL))
    h = jnp.concatenate([meta, x], axis=1)
    lb_all = jnp.cumsum(jax.nn.softmax(p['hg_lb_logits'].astype(jnp.float32), axis=0), axis=0)
    for i in range(DEPTH):
        kind, j = i % N_MIXERS, i // N_MIXERS
        if kind == 0:
            y = hgrn2_mixer(h, p['a_w_in'][j], p['a_w_out'][j], p['a_norm_g'][j], lb_all[i])
        elif kind == 1:
            y = pool_mixer(h, p['b_w_grp'][j], p['b_scale'][j])
        elif kind == 2:
            y = swa_mixer(h, p['c_w_qkv'][j], p['c_w_out'][j], p['c_sink'][j])
        else:
            y = mla_mixer(h, p['d_w_dq'][j], p['d_q_norm_g'][j], p['d_w_uq'][j], p['d_w_dkv'][j],
                          p['d_kv_norm_g'][j], p['d_w_ukv'][j], p['d_w_out'][j])
        h = layer_norm(DN_ALPHA * h + y, p['ln_g'][i, 0], p['ln_b'][i, 0])
        h = layer_norm(DN_ALPHA * h + swiglu(h, p['ffn_w_gu'][i], p['ffn_w_down'][i]), p['ln_g'][i, 1], p['ln_b'][i, 1])
    return h[:, N_META:]


def setup_inputs(seed: int = 0) -> dict:
    key = jax.random.key(seed)
    ks = jax.random.split(key, 24)
    f32 = jnp.float32
    nrm = lambda k, shape, s: jax.random.normal(k, shape, f32) * s
    HF = HG_HEADS * HG_FDIM
    HE = HG_HEADS * HG_IDIM
    return {
        'x_prompt': nrm(ks[0], (BATCH, SEQ, D_MODEL), 1.0),
        'x_sample': nrm(ks[1], (DEC_BATCH, DEC_SEQ, D_MODEL), 1.0),
        'meta_tokens': nrm(ks[2], (N_META, D_MODEL), 1.0),
        'hg_lb_logits': nrm(ks[3], (DEPTH + 1, HF), 0.1),
        'a_w_in': nrm(ks[4], (N_A, D_MODEL, 3 * HF + 2 * HE), D_MODEL ** -0.5),
        'a_w_out': nrm(ks[5], (N_A, HE, D_MODEL), DN_BETA * HE ** -0.5),
        'a_norm_g': 1.0 + nrm(ks[6], (N_A, HG_IDIM), 0.02),
        'b_w_grp': nrm(ks[7], (N_B, len(POOL_WINDOWS), POOL_GROUP, POOL_GROUP), DN_BETA * POOL_GROUP ** -0.5),
        'b_scale': 1.0 + nrm(ks[8], (N_B, D_MODEL), 0.02),
        'c_w_qkv': nrm(ks[9], (N_C, D_MODEL, (SW_HEADS + 2 * SW_KV_HEADS) * SW_HEAD_DIM), D_MODEL ** -0.5),
        'c_w_out': nrm(ks[10], (N_C, SW_HEADS * SW_HEAD_DIM, D_MODEL), DN_BETA * (SW_HEADS * SW_HEAD_DIM) ** -0.5),
        'c_sink': nrm(ks[11], (N_C, SW_HEADS), 1.0),
        'd_w_dq': nrm(ks[12], (N_D, D_MODEL, MLA_Q_RANK), D_MODEL ** -0.5),
        'd_q_norm_g': 1.0 + nrm(ks[13], (N_D, MLA_Q_RANK), 0.02),
        'd_w_uq': nrm(ks[14], (N_D, MLA_Q_RANK, MLA_HEADS * (MLA_NOPE + MLA_ROPE)), MLA_Q_RANK ** -0.5),
        'd_w_dkv': nrm(ks[15], (N_D, D_MODEL, MLA_KV_RANK + MLA_ROPE), D_MODEL ** -0.5),
        'd_kv_norm_g': 1.0 + nrm(ks[16], (N_D, MLA_KV_RANK), 0.02),
        'd_w_ukv': nrm(ks[17], (N_D, MLA_KV_RANK, MLA_HEADS * (MLA_NOPE + MLA_V)), MLA_KV_RANK ** -0.5),
        'd_w_out': nrm(ks[18], (N_D, MLA_HEADS * MLA_V, D_MODEL), DN_BETA * (MLA_HEADS * MLA_V) ** -0.5),
        'ffn_w_gu': nrm(ks[19], (DEPTH, D_MODEL, 2 * D_FF), D_MODEL ** -0.5),
        'ffn_w_down': nrm(ks[20], (DEPTH, D_FF, D_MODEL), DN_BETA * D_FF ** -0.5),
        'ln_g': 1.0 + nrm(ks[21], (DEPTH, 2, D_MODEL), 0.02),
        'ln_b': nrm(ks[22], (DEPTH, 2, D_MODEL), 0.02),
    }


def reference(x_prompt, x_sample, meta_tokens, hg_lb_logits, a_w_in, a_w_out, a_norm_g, b_w_grp, b_scale,
              c_w_qkv, c_w_out, c_sink, d_w_dq, d_q_norm_g, d_w_uq, d_w_dkv, d_kv_norm_g, d_w_ukv, d_w_out,
              ffn_w_gu, ffn_w_down, ln_g, ln_b):
    params = {
        'meta_tokens': meta_tokens, 'hg_lb_logits': hg_lb_logits,
        'a_w_in': a_w_in, 'a_w_out': a_w_out, 'a_norm_g': a_norm_g,
        'b_w_grp': b_w_grp, 'b_scale': b_scale,
        'c_w_qkv': c_w_qkv, 'c_w_out': c_w_out, 'c_sink': c_sink,
        'd_w_dq': d_w_dq, 'd_q_norm_g': d_q_norm_g, 'd_w_uq': d_w_uq, 'd_w_dkv': d_w_dkv,
        'd_kv_norm_g': d_kv_norm_g, 'd_w_ukv': d_w_ukv, 'd_w_out': d_w_out,
        'ffn_w_gu': ffn_w_gu, 'ffn_w_down': ffn_w_down, 'ln_g': ln_g, 'ln_b': ln_b,
    }
    y_prompt = trunk(x_prompt, params)
    y_sample = trunk(x_sample, params)
    return (y_prompt, y_sample)
```

```python
import functools

import jax
import jax.numpy as jnp
from jax import lax
from jax.experimental import pallas as pl
from jax.experimental.pallas import tpu as pltpu

F32 = jnp.float32
BF16 = jnp.bfloat16

D_MODEL = 1024
DEPTH = 4
N_META = 16
N_MIXERS = 4
DN_ALPHA = (2 * DEPTH) ** 0.25
LN_EPS = 1e-5
RMS_EPS = 1e-6
ROPE_THETA = 10000.0

HG_HEADS = 8
HG_FDIM = 128
HG_IDIM = D_MODEL // HG_HEADS
POOL_WINDOWS = (2, 4, 8, 16)
POOL_GROUP = D_MODEL // len(POOL_WINDOWS)
POOL_HALO = 8
SW_HEADS = 8
SW_KV_HEADS = 2
SW_GROUPS = SW_HEADS // SW_KV_HEADS
SW_HEAD_DIM = D_MODEL // SW_HEADS
SW_WINDOW = 128
MLA_HEADS = 16
MLA_Q_RANK = 256
MLA_KV_RANK = 256
MLA_NOPE = 128
MLA_ROPE = 64
MLA_V = 128
MLA_QK = 256
D_FF = ((8 * D_MODEL + 3 * 256 - 1) // (3 * 256)) * 256

LANE = 128
HG_CHUNK = 128
VMEM_LIMIT_BYTES = 56 * 1024 * 1024
KEY_MASK_BIAS = -1e30

NT_DIMS = (((1,), (1,)), ((), ()))


def _cparams(*sem):
    return pltpu.CompilerParams(dimension_semantics=sem, vmem_limit_bytes=VMEM_LIMIT_BYTES)


def _row_tile(tp, max_rows):
    best = LANE
    for t in range(LANE, max_rows + 1, LANE):
        if tp % t == 0:
            best = t
    return best


def _whole(shape):
    nd = len(shape)
    return pl.BlockSpec(shape, lambda *_: (0,) * nd, pipeline_mode=pl.Buffered(1))


def _rows(tm, width, col=0):
    return pl.BlockSpec((1, tm, width), lambda b, i: (b, i, col))


def _layer_norm(z, g, b):
    mu = jnp.mean(z, axis=-1, keepdims=True)
    zc = z - mu
    var = jnp.mean(zc * zc, axis=-1, keepdims=True)
    return zc * lax.rsqrt(var + LN_EPS) * g + b


def _rms_norm(x, g):
    return x * lax.rsqrt(jnp.mean(x * x, axis=-1, keepdims=True) + RMS_EPS) * g


def _sigmoid(x):
    return 1.0 / (1.0 + jnp.exp(-x))


def _silu(x):
    return x * _sigmoid(x)


def _bdot(a, b):
    return jnp.dot(a, b, preferred_element_type=F32)


def _matmul_kernel(x_ref, w_ref, o_ref):
    o_ref[0] = _bdot(x_ref[0].astype(BF16), w_ref[...])


def _matmul(x, w, max_rows):
    B, Tp, K = x.shape
    N = w.shape[1]
    tm = _row_tile(Tp, max_rows)
    return pl.pallas_call(
        _matmul_kernel,
        grid=(B, Tp // tm),
        in_specs=[_rows(tm, K), _whole((K, N))],
        out_specs=_rows(tm, N),
        out_shape=jax.ShapeDtypeStruct((B, Tp, N), F32),
        compiler_params=_cparams("parallel", "parallel"),
        name="in_proj",
    )(x, w)


def _proj_ln_kernel(a_ref, w_ref, h_ref, g_ref, b_ref, o_ref):
    y = _bdot(a_ref[0], w_ref[...])
    o_ref[0] = _layer_norm(DN_ALPHA * h_ref[0] + y, g_ref[...], b_ref[...])


def _proj_ln(a, w, h, g, b, max_rows=640):
    B, Tp, K = a.shape
    tm = _row_tile(Tp, max_rows)
    return pl.pallas_call(
        _proj_ln_kernel,
        grid=(B, Tp // tm),
        in_specs=[_rows(tm, K), _whole((K, D_MODEL)), _rows(tm, D_MODEL),
                  _whole((1, D_MODEL)), _whole((1, D_MODEL))],
        out_specs=_rows(tm, D_MODEL),
        out_shape=jax.ShapeDtypeStruct((B, Tp, D_MODEL), F32),
        compiler_params=_cparams("parallel", "parallel"),
        name="out_proj_ln",
    )(a, w, h, g, b)


FFN_CHUNKS = 2


def _ffn_kernel(x_ref, wgu_ref, wd_ref, g_ref, b_ref, o_ref):
    x = x_ref[0]
    xb = x.astype(BF16)
    ch = D_FF // FFN_CHUNKS
    y = None
    for c in range(FFN_CHUNKS):
        gate = _bdot(xb, wgu_ref[:, c * ch:(c + 1) * ch])
        up = _bdot(xb, wgu_ref[:, D_FF + c * ch:D_FF + (c + 1) * ch])
        act = (_silu(gate) * up).astype(BF16)
        part = _bdot(act, wd_ref[c * ch:(c + 1) * ch, :])
        y = part if y is None else y + part
    o_ref[0] = _layer_norm(DN_ALPHA * x + y, g_ref[...], b_ref[...])


def _ffn(h, wgu, wd, g, b):
    B, Tp, _ = h.shape
    tm = _row_tile(Tp, 640)
    return pl.pallas_call(
        _ffn_kernel,
        grid=(B, Tp // tm),
        in_specs=[_rows(tm, D_MODEL), _whole((D_MODEL, 2 * D_FF)), _whole((D_FF, D_MODEL)),
                  _whole((1, D_MODEL)), _whole((1, D_MODEL))],
        out_specs=_rows(tm, D_MODEL),
        out_shape=jax.ShapeDtypeStruct((B, Tp, D_MODEL), F32),
        compiler_params=_cparams("parallel", "parallel"),
        name="ffn",
    )(h, wgu, wd, g, b)


def _hgrn_scan_kernel(q_ref, v_ref, a_ref, lb_ref, o_ref, st_ref, *, reverse, n_chunks, pad):
    C = HG_CHUNK
    c = pl.program_id(1)

    @pl.when(c == 0)
    def _():
        st_ref[...] = jnp.zeros_like(st_ref)

    chunk = (n_chunks - 1 - c) if reverse else c
    row = lax.broadcasted_iota(jnp.int32, (C, C), 0)
    col = lax.broadcasted_iota(jnp.int32, (C, C), 1)
    t1 = lax.broadcasted_iota(jnp.int32, (C, 1), 0)
    rb, cb = row // 32, col // 32
    if reverse:
        cum = col >= row
        m1 = (row < 64) & (col >= 64)
        m2 = (rb // 2 == cb // 2) & (rb % 2 == 0) & (cb % 2 == 1)
        m3 = (rb == cb) & (col >= row)
        ref1, ref2, ref3, last = 64, (32, 96), (16, 48, 80, 112), 0
    else:
        cum = col <= row
        m1 = (row >= 64) & (col < 64)
        m2 = (rb // 2 == cb // 2) & (rb % 2 == 1) & (cb % 2 == 0)
        m3 = (rb == cb) & (col <= row)
        ref1, ref2, ref3, last = 63, (31, 95), (15, 47, 79, 111), C - 1
    cum = cum.astype(BF16)
    cum3 = jnp.concatenate([cum, cum, cum], axis=1)
    valid = (chunk * C + t1) >= pad

    for h in range(HG_HEADS):
        sl = slice(h * HG_FDIM, (h + 1) * HG_FDIM)
        q = _silu(q_ref[0, :, sl])
        v = v_ref[0, :, sl]
        a = a_ref[0, :, sl]
        lb = lb_ref[:, sl]
        f = lb + (1.0 - lb) * _sigmoid(a)
        k = jnp.where(valid, (1.0 - lb) * _sigmoid(-a), 0.0)
        lf = jnp.where(valid, jnp.log(f), 0.0)
        lf0 = lf.astype(BF16)
        rem = lf - lf0.astype(F32)
        lf1 = rem.astype(BF16)
        lf2 = (rem - lf1.astype(F32)).astype(BF16)
        b = _bdot(cum3, jnp.concatenate([lf0, lf1, lf2], axis=0))

        def brow(i):
            return b[i:i + 1, :]

        r1 = brow(ref1)
        r2 = jnp.where(t1 < 64, brow(ref2[0]), brow(ref2[1]))
        r3 = jnp.where(t1 < 32, brow(ref3[0]),
                       jnp.where(t1 < 64, brow(ref3[1]),
                                 jnp.where(t1 < 96, brow(ref3[2]), brow(ref3[3]))))
        btot = brow(last)
        d1, d2, d3 = b - r1, b - r2, b - r3
        p1 = lax.dot_general((q * jnp.exp(jnp.minimum(d1, 0.0))).astype(BF16),
                             (k * jnp.exp(jnp.minimum(-d1, 0.0))).astype(BF16),
                             NT_DIMS, preferred_element_type=F32)
        p2 = lax.dot_general((q * jnp.exp(jnp.minimum(d2, 0.0))).astype(BF16),
                             (k * jnp.exp(jnp.minimum(-d2, 0.0))).astype(BF16),
                             NT_DIMS, preferred_element_type=F32)
        p3 = lax.dot_general((q * jnp.exp(d3)).astype(BF16), (k * jnp.exp(-d3)).astype(BF16),
                             NT_DIMS, preferred_element_type=F32)
        att = jnp.where(m1, p1, jnp.where(m2, p2, jnp.where(m3, p3, 0.0)))
        st = st_ref[h]
        o = _bdot(att.astype(BF16), v.astype(BF16))
        o = o + lax.dot_general((q * jnp.exp(b)).astype(BF16), st.astype(BF16),
                                NT_DIMS, preferred_element_type=F32)
        o_ref[0, :, sl] = o
        ks = (k * jnp.exp(btot - b)).astype(BF16)
        st_ref[h] = st * jnp.exp(btot) + _bdot(v.T.astype(BF16), ks)


def _hgrn_scan(proj, lb, reverse, pad):
    B, Tp, _ = proj.shape
    C = HG_CHUNK
    n = Tp // C
    width = HG_HEADS * HG_FDIM

    def spec(colblk):
        if reverse:
            return pl.BlockSpec((1, C, width), lambda b, c: (b, n - 1 - c, colblk))
        return pl.BlockSpec((1, C, width), lambda b, c: (b, c, colblk))

    return pl.pallas_call(
        functools.partial(_hgrn_scan_kernel, reverse=reverse, n_chunks=n, pad=pad),
        grid=(B, n),
        in_specs=[spec(0), spec(1), spec(3 if reverse else 2), _whole((1, width))],
        out_specs=spec(0),
        out_shape=jax.ShapeDtypeStruct((B, Tp, width), F32),
        scratch_shapes=[pltpu.VMEM((HG_HEADS, HG_IDIM, HG_FDIM), F32)],
        compiler_params=_cparams("parallel", "arbitrary"),
        name="hgrn_scan_bwd" if reverse else "hgrn_scan_fwd",
    )(proj, proj, proj, lb)


def _hgrn_out_kernel(of_ref, ob_ref, gate_ref, ng_ref, w_ref, h_ref, g_ref, b_ref, o_ref):
    o = of_ref[0] + ob_ref[0]
    parts = []
    for hd in range(HG_HEADS):
        sl = slice(hd * HG_IDIM, (hd + 1) * HG_IDIM)
        parts.append(_rms_norm(o[:, sl], ng_ref[...]))
    on = jnp.concatenate(parts, axis=1) * _silu(gate_ref[0])
    y = _bdot(on.astype(BF16), w_ref[...])
    o_ref[0] = _layer_norm(DN_ALPHA * h_ref[0] + y, g_ref[...], b_ref[...])


def _hgrn_out(o_f, o_b, proj, norm_g, w_out, h, g, b):
    B, Tp, _ = h.shape
    tm = _row_tile(Tp, 640)
    return pl.pallas_call(
        _hgrn_out_kernel,
        grid=(B, Tp // tm),
        in_specs=[_rows(tm, D_MODEL), _rows(tm, D_MODEL), _rows(tm, D_MODEL, col=4),
                  _whole((1, HG_IDIM)), _whole((D_MODEL, D_MODEL)), _rows(tm, D_MODEL),
                  _whole((1, D_MODEL)), _whole((1, D_MODEL))],
        out_specs=_rows(tm, D_MODEL),
        out_shape=jax.ShapeDtypeStruct((B, Tp, D_MODEL), F32),
        compiler_params=_cparams("parallel", "parallel"),
        name="hgrn_out",
    )(o_f, o_b, proj, norm_g, w_out, h, g, b)


def _hgrn_mixer(h, pad, w_in, w_out, norm_g, lb, g, b):
    proj = _matmul(h, w_in, 640)
    o_f = _hgrn_scan(proj, lb, False, pad)
    o_b = _hgrn_scan(proj, lb, True, pad)
    return _hgrn_out(o_f, o_b, proj, norm_g, w_out, h, g, b)


def _pool_kernel(xp_ref, x_ref, xn_ref, w_ref, sc_ref, g_ref, b_ref, o_ref, xe_ref, *, tt, pad, t_len):
    i = pl.program_id(1)
    H = POOL_HALO
    x = x_ref[0]

    def keep(vals, first_row):
        r = first_row + lax.broadcasted_iota(jnp.int32, (vals.shape[0], 1), 0)
        return jnp.where((r >= pad) & (r < pad + t_len), vals, 0.0)

    xe_ref[0:H, :] = keep(xp_ref[0], i * tt - H)
    xe_ref[H:H + tt, :] = keep(x, i * tt)
    xe_ref[H + tt:H + tt + H, :] = keep(xn_ref[0], i * tt + tt)

    tok = i * tt - pad + lax.broadcasted_iota(jnp.int32, (tt, 1), 0)
    ys = []
    for gi, w in enumerate(POOL_WINDOWS):
        sl = slice(gi * POOL_GROUP, (gi + 1) * POOL_GROUP)
        acc = None
        for j in range(-(w // 2), w // 2):
            piece = xe_ref[H + j:H + j + tt, sl]
            acc = piece if acc is None else acc + piece
        lo = jnp.clip(tok - w // 2, 0, t_len)
        hi = jnp.clip(tok + w // 2, 0, t_len)
        cnt = jnp.maximum(hi - lo, 1).astype(F32)
        p = acc / cnt - x[:, sl]
        ys.append(_bdot(p.astype(BF16), w_ref[gi]))
    y = jnp.concatenate(ys, axis=1) * sc_ref[...]
    o_ref[0] = _layer_norm(DN_ALPHA * x + y, g_ref[...], b_ref[...])


def _pool_mixer(h, pad, w_grp, scale, g, b):
    B, Tp, _ = h.shape
    tt = _row_tile(Tp, 640)
    H = POOL_HALO
    per = tt // H
    last = Tp // H - 1
    return pl.pallas_call(
        functools.partial(_pool_kernel, tt=tt, pad=pad, t_len=Tp - pad),
        grid=(B, Tp // tt),
        in_specs=[pl.BlockSpec((1, H, D_MODEL), lambda b, i: (b, jnp.maximum(i * per - 1, 0), 0)),
                  _rows(tt, D_MODEL),
                  pl.BlockSpec((1, H, D_MODEL), lambda b, i: (b, jnp.minimum((i + 1) * per, last), 0)),
                  _whole((len(POOL_WINDOWS), POOL_GROUP, POOL_GROUP)), _whole((1, D_MODEL)),
                  _whole((1, D_MODEL)), _whole((1, D_MODEL))],
        out_specs=_rows(tt, D_MODEL),
        out_shape=jax.ShapeDtypeStruct((B, Tp, D_MODEL), F32),
        scratch_shapes=[pltpu.VMEM((tt + 2 * H, D_MODEL), F32)],
        compiler_params=_cparams("parallel", "parallel"),
        name="pool_mixer",
    )(h, h, h, w_grp, scale, g, b)


def _rope_tables(tp, pad, d):
    inv = 1.0 / (ROPE_THETA ** (jnp.arange(0, d, 2, dtype=F32) / d))
    pos = (jnp.arange(tp) - pad).astype(F32)
    ang = pos[:, None] * inv[None, :]
    cos, sin = jnp.cos(ang), jnp.sin(ang)
    reps = LANE // d
    return (jnp.tile(jnp.concatenate([cos, cos], axis=1), (1, reps)),
            jnp.tile(jnp.concatenate([-sin, sin], axis=1), (1, reps)))


def _swa_proj_kernel(x_ref, w_ref, cos_ref, sin_ref, q_ref, k_ref, v_ref):
    qkv = _bdot(x_ref[0].astype(BF16), w_ref[...])
    cos, sin = cos_ref[...], sin_ref[...]
    nq = SW_HEADS * SW_HEAD_DIM
    nk = SW_KV_HEADS * SW_HEAD_DIM

    def rope(xh):
        return xh * cos + pltpu.roll(xh, SW_HEAD_DIM // 2, 1) * sin

    for hd in range(SW_HEADS):
        sl = slice(hd * SW_HEAD_DIM, (hd + 1) * SW_HEAD_DIM)
        q_ref[0, :, sl] = rope(qkv[:, sl]).astype(BF16)
    for hd in range(SW_KV_HEADS):
        sl = slice(hd * SW_HEAD_DIM, (hd + 1) * SW_HEAD_DIM)
        k_ref[0, :, sl] = rope(qkv[:, nq + hd * SW_HEAD_DIM:nq + (hd + 1) * SW_HEAD_DIM]).astype(BF16)
    v_ref[0] = qkv[:, nq + nk:].astype(BF16)


def _swa_attn_kernel(sink_ref, q_ref, kp_ref, kc_ref, kn_ref, vp_ref, vc_ref, vn_ref,
                     w_ref, h_ref, g_ref, b_ref, o_ref, *, pad, tp):
    i = pl.program_id(1)
    blk = SW_WINDOW
    q = q_ref[0]
    kcat = jnp.concatenate([kp_ref[0], kc_ref[0], kn_ref[0]], axis=0)
    vcat = jnp.concatenate([vp_ref[0], vc_ref[0], vn_ref[0]], axis=0)
    rq = i * blk + lax.broadcasted_iota(jnp.int32, (blk, 3 * blk), 0)
    rk = (i - 1) * blk + lax.broadcasted_iota(jnp.int32, (blk, 3 * blk), 1)
    ok = (rk >= pad) & (rk < tp) & (jnp.abs(rq - rk) <= SW_WINDOW)
    scale = SW_HEAD_DIM ** -0.5
    outs = []
    for hd in range(SW_HEADS):
        kv = hd // SW_GROUPS
        ksl = slice(kv * SW_HEAD_DIM, (kv + 1) * SW_HEAD_DIM)
        s = lax.dot_general(q[:, hd * SW_HEAD_DIM:(hd + 1) * SW_HEAD_DIM], kcat[:, ksl],
                            NT_DIMS, preferred_element_type=F32) * scale
        s = jnp.where(ok, s, -jnp.inf)
        sk = sink_ref[hd]
        m = jnp.maximum(jnp.max(s, axis=-1, keepdims=True), sk)
        p = jnp.exp(s - m)
        den = jnp.sum(p, axis=-1, keepdims=True) + jnp.exp(sk - m)
        outs.append(_bdot((p * (1.0 / den)).astype(BF16), vcat[:, ksl]))
    o = jnp.concatenate(outs, axis=1).astype(BF16)
    y = _bdot(o, w_ref[...])
    o_ref[0] = _layer_norm(DN_ALPHA * h_ref[0] + y, g_ref[...], b_ref[...])


def _swa_mixer(h, pad, w_qkv, w_out, sink, g, b):
    B, Tp, _ = h.shape
    cos, sin = _rope_tables(Tp, pad, SW_HEAD_DIM)
    tm = _row_tile(Tp, 640)
    nq = SW_HEADS * SW_HEAD_DIM
    nk = SW_KV_HEADS * SW_HEAD_DIM
    tab = pl.BlockSpec((tm, LANE), lambda b, i: (i, 0))
    q, k, v = pl.pallas_call(
        _swa_proj_kernel,
        grid=(B, Tp // tm),
        in_specs=[_rows(tm, D_MODEL), _whole((D_MODEL, nq + 2 * nk)), tab, tab],
        out_specs=[_rows(tm, nq), _rows(tm, nk), _rows(tm, nk)],
        out_shape=[jax.ShapeDtypeStruct((B, Tp, nq), BF16), jax.ShapeDtypeStruct((B, Tp, nk), BF16),
                   jax.ShapeDtypeStruct((B, Tp, nk), BF16)],
        compiler_params=_cparams("parallel", "parallel"),
        name="swa_proj",
    )(h, w_qkv, cos, sin)

    blk = SW_WINDOW
    nb = Tp // blk
    prev = pl.BlockSpec((1, blk, nk), lambda b, i: (b, jnp.maximum(i - 1, 0), 0))
    cur = pl.BlockSpec((1, blk, nk), lambda b, i: (b, i, 0))
    nxt = pl.BlockSpec((1, blk, nk), lambda b, i: (b, jnp.minimum(i + 1, nb - 1), 0))
    return pl.pallas_call(
        functools.partial(_swa_attn_kernel, pad=pad, tp=Tp),
        grid=(B, nb),
        in_specs=[pl.BlockSpec(memory_space=pltpu.SMEM),
                  _rows(blk, nq), prev, cur, nxt, prev, cur, nxt,
                  _whole((nq, D_MODEL)), _rows(blk, D_MODEL),
                  _whole((1, D_MODEL)), _whole((1, D_MODEL))],
        out_specs=_rows(blk, D_MODEL),
        out_shape=jax.ShapeDtypeStruct((B, Tp, D_MODEL), F32),
        compiler_params=_cparams("parallel", "parallel"),
        name="swa_attn",
    )(sink, q, k, k, k, v, v, v, w_out, h, g, b)


def _mla_proj_kernel(x_ref, wdq_ref, gq_ref, wuq_ref, wdkv_ref, gkv_ref, wukv_ref, cos_ref, sin_ref,
                     q_ref, k_ref, v_ref, *, pad, tm):
    i = pl.program_id(1)
    xb = x_ref[0].astype(BF16)
    cq = _rms_norm(_bdot(xb, wdq_ref[...]), gq_ref[...])
    q = _bdot(cq.astype(BF16), wuq_ref[...])
    ckv = _bdot(xb, wdkv_ref[...])
    c = _rms_norm(ckv[:, :MLA_KV_RANK], gkv_ref[...])
    kv = _bdot(c.astype(BF16), wukv_ref[...])
    cos, sin = cos_ref[...], sin_ref[...]
    lane = lax.broadcasted_iota(jnp.int32, (tm, LANE), 1)
    low = (lane % MLA_ROPE) < MLA_ROPE // 2
    spare = lane == MLA_ROPE

    def rope(xr):
        swapped = jnp.where(low, pltpu.roll(xr, LANE - MLA_ROPE // 2, 1), pltpu.roll(xr, MLA_ROPE // 2, 1))
        return xr * cos + swapped * sin

    filler = (i * tm + lax.broadcasted_iota(jnp.int32, (tm, 1), 0)) < pad
    kr = jnp.where(spare, jnp.where(filler, KEY_MASK_BIAS, 0.0), rope(ckv[:, MLA_KV_RANK:])).astype(BF16)
    for hd in range(MLA_HEADS):
        base = hd * MLA_QK
        q_ref[0, :, base:base + MLA_NOPE] = q[:, base:base + MLA_NOPE].astype(BF16)
        qr = jnp.where(spare, 1.0, rope(q[:, base + MLA_NOPE:base + MLA_QK]))
        q_ref[0, :, base + MLA_NOPE:base + MLA_QK] = qr.astype(BF16)
        k_ref[0, :, base:base + MLA_NOPE] = kv[:, hd * MLA_NOPE:(hd + 1) * MLA_NOPE].astype(BF16)
        k_ref[0, :, base + MLA_NOPE:base + MLA_QK] = kr
    v_ref[0] = kv[:, MLA_HEADS * MLA_NOPE:].astype(BF16)


def _mla_attn_kernel(q_ref, k_ref, v_ref, o_ref):
    scale = (MLA_NOPE + MLA_ROPE) ** -0.5
    s = lax.dot_general(q_ref[0], k_ref[0], NT_DIMS, preferred_element_type=F32) * scale
    m = jnp.max(s, axis=-1, keepdims=True)
    p = jnp.exp(s - m)
    den = jnp.sum(p, axis=-1, keepdims=True)
    o_ref[0] = _bdot((p * (1.0 / den)).astype(BF16), v_ref[0]).astype(BF16)


def _mla_mixer(h, pad, w_dq, q_norm_g, w_uq, w_dkv, kv_norm_g, w_ukv, w_out, g, b):
    B, Tp, _ = h.shape
    cos, sin = _rope_tables(Tp, pad, MLA_ROPE)
    tm = _row_tile(Tp, 384)
    nqk = MLA_HEADS * MLA_QK
    nv = MLA_HEADS * MLA_V
    tab = pl.BlockSpec((tm, LANE), lambda b, i: (i, 0))
    q, k, v = pl.pallas_call(
        functools.partial(_mla_proj_kernel, pad=pad, tm=tm),
        grid=(B, Tp // tm),
        in_specs=[_rows(tm, D_MODEL), _whole(w_dq.shape), _whole((1, MLA_Q_RANK)), _whole(w_uq.shape),
                  _whole(w_dkv.shape), _whole((1, MLA_KV_RANK)), _whole(w_ukv.shape), tab, tab],
        out_specs=[_rows(tm, nqk), _rows(tm, nqk), _rows(tm, nv)],
        out_shape=[jax.ShapeDtypeStruct((B, Tp, nqk), BF16), jax.ShapeDtypeStruct((B, Tp, nqk), BF16),
                   jax.ShapeDtypeStruct((B, Tp, nv), BF16)],
        compiler_params=_cparams("parallel", "parallel"),
        name="mla_proj",
    )(h, w_dq, q_norm_g, w_uq, w_dkv, kv_norm_g, w_ukv, cos, sin)

    tq = _row_tile(Tp, 384 if Tp <= 4608 else 128)
    o = pl.pallas_call(
        _mla_attn_kernel,
        grid=(B, MLA_HEADS, Tp // tq),
        in_specs=[pl.BlockSpec((1, tq, MLA_QK), lambda b, hd, i: (b, i, hd)),
                  pl.BlockSpec((1, Tp, MLA_QK), lambda b, hd, i: (b, 0, hd)),
                  pl.BlockSpec((1, Tp, MLA_V), lambda b, hd, i: (b, 0, hd))],
        out_specs=pl.BlockSpec((1, tq, MLA_V), lambda b, hd, i: (b, i, hd)),
        out_shape=jax.ShapeDtypeStruct((B, Tp, nv), BF16),
        compiler_params=_cparams("parallel", "parallel", "parallel"),
        name="mla_attn",
    )(q, k, v)
    return _proj_ln(o, w_out, h, g, b)


def _prepare_params(p):
    bf = lambda a: a.astype(BF16)
    row = lambda a: a.astype(F32).reshape(1, -1)
    out = dict(p)
    for name in ('a_w_in', 'a_w_out', 'b_w_grp', 'c_w_qkv', 'c_w_out', 'd_w_dq', 'd_w_out',
                 'ffn_w_gu', 'ffn_w_down'):
        out[name] = bf(p[name])
    n_d = p['d_w_uq'].shape[0]
    wuq = p['d_w_uq'].reshape(n_d, MLA_Q_RANK, MLA_HEADS, MLA_NOPE + MLA_ROPE)
    wuq = jnp.pad(wuq, ((0, 0), (0, 0), (0, 0), (0, MLA_QK - MLA_NOPE - MLA_ROPE)))
    out['d_w_uq'] = bf(wuq.reshape(n_d, MLA_Q_RANK, MLA_HEADS * MLA_QK))
    out['d_w_dkv'] = bf(jnp.pad(p['d_w_dkv'], ((0, 0), (0, 0), (0, LANE - MLA_ROPE))))
    wukv = p['d_w_ukv'].reshape(n_d, MLA_KV_RANK, MLA_HEADS, MLA_NOPE + MLA_V)
    out['d_w_ukv'] = bf(jnp.concatenate(
        [wukv[..., :MLA_NOPE].reshape(n_d, MLA_KV_RANK, MLA_HEADS * MLA_NOPE),
         wukv[..., MLA_NOPE:].reshape(n_d, MLA_KV_RANK, MLA_HEADS * MLA_V)], axis=-1))
    out['lb_all'] = jnp.cumsum(jax.nn.softmax(p['hg_lb_logits'].astype(F32), axis=0), axis=0)
    return out


def _trunk(x, p):
    B, S, _ = x.shape
    T = N_META + S
    Tp = -(-T // LANE) * LANE
    pad = Tp - T
    meta = jnp.broadcast_to(p['meta_tokens'].astype(x.dtype)[None], (B, N_META, D_MODEL))
    h = jnp.concatenate([jnp.zeros((B, pad, D_MODEL), x.dtype), meta, x], axis=1)
    row = lambda a: a.astype(F32).reshape(1, -1)
    for i in range(DEPTH):
        kind, j = i % N_MIXERS, i // N_MIXERS
        g0, b0 = row(p['ln_g'][i, 0]), row(p['ln_b'][i, 0])
        if kind == 0:
            h = _hgrn_mixer(h, pad, p['a_w_in'][j], p['a_w_out'][j], row(p['a_norm_g'][j]),
                            row(p['lb_all'][i]), g0, b0)
        elif kind == 1:
            h = _pool_mixer(h, pad, p['b_w_grp'][j], row(p['b_scale'][j]), g0, b0)
        elif kind == 2:
            h = _swa_mixer(h, pad, p['c_w_qkv'][j], p['c_w_out'][j], p['c_sink'][j].astype(F32), g0, b0)
        else:
            h = _mla_mixer(h, pad, p['d_w_dq'][j], row(p['d_q_norm_g'][j]), p['d_w_uq'][j],
                           p['d_w_dkv'][j], row(p['d_kv_norm_g'][j]), p['d_w_ukv'][j],
                           p['d_w_out'][j], g0, b0)
        h = _ffn(h, p['ffn_w_gu'][i], p['ffn_w_down'][i], row(p['ln_g'][i, 1]), row(p['ln_b'][i, 1]))
    return h[:, pad + N_META:]


def kernel(x_prompt, x_sample, meta_tokens, hg_lb_logits, a_w_in, a_w_out, a_norm_g, b_w_grp, b_scale,
           c_w_qkv, c_w_out, c_sink, d_w_dq, d_q_norm_g, d_w_uq, d_w_dkv, d_kv_norm_g, d_w_ukv, d_w_out,
           ffn_w_gu, ffn_w_down, ln_g, ln_b):
    params = _prepare_params({
        'meta_tokens': meta_tokens, 'hg_lb_logits': hg_lb_logits,
        'a_w_in': a_w_in, 'a_w_out': a_w_out, 'a_norm_g': a_norm_g,
        'b_w_grp': b_w_grp, 'b_scale': b_scale,
        'c_w_qkv': c_w_qkv, 'c_w_out': c_w_out, 'c_sink': c_sink,
        'd_w_dq': d_w_dq, 'd_q_norm_g': d_q_norm_g, 'd_w_uq': d_w_uq, 'd_w_dkv': d_w_dkv,
        'd_kv_norm_g': d_kv_norm_g, 'd_w_ukv': d_w_ukv, 'd_w_out': d_w_out,
        'ffn_w_gu': ffn_w_gu, 'ffn_w_down': ffn_w_down, 'ln_g': ln_g, 'ln_b': ln_b,
    })
    return (_trunk(x_prompt, params), _trunk(x_sample, params))
```

```python
import functools

import jax
import jax.numpy as jnp
from jax import lax
from jax.experimental import pallas as pl
from jax.experimental.pallas import tpu as pltpu

F32 = jnp.float32
BF16 = jnp.bfloat16

D_MODEL = 1024
DEPTH = 4
N_META = 16
N_MIXERS = 4
DN_ALPHA = (2 * DEPTH) ** 0.25
LN_EPS = 1e-5
RMS_EPS = 1e-6
ROPE_THETA = 10000.0

HG_HEADS = 8
HG_FDIM = 128
HG_IDIM = D_MODEL // HG_HEADS
POOL_WINDOWS = (2, 4, 8, 16)
POOL_GROUP = D_MODEL // len(POOL_WINDOWS)
POOL_HALO = 8
SW_HEADS = 8
SW_KV_HEADS = 2
SW_GROUPS = SW_HEADS // SW_KV_HEADS
SW_HEAD_DIM = D_MODEL // SW_HEADS
SW_WINDOW = 128
MLA_HEADS = 16
MLA_Q_RANK = 256
MLA_KV_RANK = 256
MLA_NOPE = 128
MLA_ROPE = 64
MLA_V = 128
MLA_QK = 256
D_FF = ((8 * D_MODEL + 3 * 256 - 1) // (3 * 256)) * 256

LANE = 128
HG_CHUNK = 128
VMEM_LIMIT_BYTES = 56 * 1024 * 1024
KEY_MASK_BIAS = -1e30
LOG2_E = 1.4426950408889634

NT_DIMS = (((1,), (1,)), ((), ()))


def _cparams(*sem):
    return pltpu.CompilerParams(dimension_semantics=sem, vmem_limit_bytes=VMEM_LIMIT_BYTES)


def _row_tile(tp, max_rows):
    best = LANE
    for t in range(LANE, max_rows + 1, LANE):
        if tp % t == 0:
            best = t
    return best


def _whole(shape):
    nd = len(shape)
    return pl.BlockSpec(shape, lambda *_: (0,) * nd, pipeline_mode=pl.Buffered(1))


def _rows(tm, width, col=0):
    return pl.BlockSpec((1, tm, width), lambda b, i: (b, i, col))


def _layer_norm(z, g, b):
    mu = jnp.mean(z, axis=-1, keepdims=True)
    zc = z - mu
    var = jnp.mean(zc * zc, axis=-1, keepdims=True)
    return zc * lax.rsqrt(var + LN_EPS) * g + b


def _rms_norm(x, g):
    return x * lax.rsqrt(jnp.mean(x * x, axis=-1, keepdims=True) + RMS_EPS) * g


def _sigmoid(x):
    return 1.0 / (1.0 + jnp.exp(-x))


def _silu(x):
    return x * _sigmoid(x)


def _bdot(a, b):
    return jnp.dot(a, b, preferred_element_type=F32)


def _matmul_kernel(x_ref, w_ref, o_ref):
    o_ref[0] = _bdot(x_ref[0].astype(BF16), w_ref[...])


def _matmul(x, w, max_rows):
    B, Tp, K = x.shape
    N = w.shape[1]
    tm = _row_tile(Tp, max_rows)
    return pl.pallas_call(
        _matmul_kernel,
        grid=(B, Tp // tm),
        in_specs=[_rows(tm, K), _whole((K, N))],
        out_specs=_rows(tm, N),
        out_shape=jax.ShapeDtypeStruct((B, Tp, N), F32),
        compiler_params=_cparams("parallel", "parallel"),
        name="in_proj",
    )(x, w)


def _proj_ln_kernel(a_ref, w_ref, h_ref, g_ref, b_ref, o_ref):
    y = _bdot(a_ref[0], w_ref[...])
    o_ref[0] = _layer_norm(DN_ALPHA * h_ref[0] + y, g_ref[...], b_ref[...])


def _proj_ln(a, w, h, g, b, max_rows=640):
    B, Tp, K = a.shape
    tm = _row_tile(Tp, max_rows)
    return pl.pallas_call(
        _proj_ln_kernel,
        grid=(B, Tp // tm),
        in_specs=[_rows(tm, K), _whole((K, D_MODEL)), _rows(tm, D_MODEL),
                  _whole((1, D_MODEL)), _whole((1, D_MODEL))],
        out_specs=_rows(tm, D_MODEL),
        out_shape=jax.ShapeDtypeStruct((B, Tp, D_MODEL), F32),
        compiler_params=_cparams("parallel", "parallel"),
        name="out_proj_ln",
    )(a, w, h, g, b)


FFN_CHUNKS = 2


def _ffn_kernel(x_ref, wgu_ref, wd_ref, g_ref, b_ref, o_ref):
    x = x_ref[0]
    xb = x.astype(BF16)
    ch = D_FF // FFN_CHUNKS
    y = None
    for c in range(FFN_CHUNKS):
        gate = _bdot(xb, wgu_ref[:, c * ch:(c + 1) * ch])
        up = _bdot(xb, wgu_ref[:, D_FF + c * ch:D_FF + (c + 1) * ch])
        act = (_silu(gate) * up).astype(BF16)
        part = _bdot(act, wd_ref[c * ch:(c + 1) * ch, :])
        y = part if y is None else y + part
    o_ref[0] = _layer_norm(DN_ALPHA * x + y, g_ref[...], b_ref[...])


def _ffn(h, wgu, wd, g, b):
    B, Tp, _ = h.shape
    tm = _row_tile(Tp, 640)
    return pl.pallas_call(
        _ffn_kernel,
        grid=(B, Tp // tm),
        in_specs=[_rows(tm, D_MODEL), _whole((D_MODEL, 2 * D_FF)), _whole((D_FF, D_MODEL)),
                  _whole((1, D_MODEL)), _whole((1, D_MODEL))],
        out_specs=_rows(tm, D_MODEL),
        out_shape=jax.ShapeDtypeStruct((B, Tp, D_MODEL), F32),
        compiler_params=_cparams("parallel", "parallel"),
        name="ffn",
    )(h, wgu, wd, g, b)


def _hgrn_scan_kernel(q_ref, v_ref, a_ref, lb_ref, o_ref, st_ref, *, reverse, n_chunks, pad):
    C = HG_CHUNK
    c = pl.program_id(1)

    @pl.when(c == 0)
    def _():
        st_ref[...] = jnp.zeros_like(st_ref)

    chunk = (n_chunks - 1 - c) if reverse else c
    row = lax.broadcasted_iota(jnp.int32, (C, C), 0)
    col = lax.broadcasted_iota(jnp.int32, (C, C), 1)
    t1 = lax.broadcasted_iota(jnp.int32, (C, 1), 0)
    rb, cb = row // 32, col // 32
    if reverse:
        cum = col >= row
        m1 = (row < 64) & (col >= 64)
        m2 = (rb // 2 == cb // 2) & (rb % 2 == 0) & (cb % 2 == 1)
        m3 = (rb == cb) & (col >= row)
        ref1, ref2, ref3, last = 64, (32, 96), (16, 48, 80, 112), 0
    else:
        cum = col <= row
        m1 = (row >= 64) & (col < 64)
        m2 = (rb // 2 == cb // 2) & (rb % 2 == 1) & (cb % 2 == 0)
        m3 = (rb == cb) & (col <= row)
        ref1, ref2, ref3, last = 63, (31, 95), (15, 47, 79, 111), C - 1
    cum = cum.astype(BF16)
    cum3 = jnp.concatenate([cum, cum, cum], axis=1)
    valid = (chunk * C + t1) >= pad

    for h in range(HG_HEADS):
        sl = slice(h * HG_FDIM, (h + 1) * HG_FDIM)
        q = _silu(q_ref[0, :, sl])
        v = v_ref[0, :, sl]
        a = a_ref[0, :, sl]
        lb = lb_ref[:, sl]
        f = lb + (1.0 - lb) * _sigmoid(a)
        k = jnp.where(valid, (1.0 - lb) * _sigmoid(-a), 0.0)
        lf = jnp.where(valid, jnp.log(f), 0.0)
        lf0 = lf.astype(BF16)
        rem = lf - lf0.astype(F32)
        lf1 = rem.astype(BF16)
        lf2 = (rem - lf1.astype(F32)).astype(BF16)
        b = _bdot(cum3, jnp.concatenate([lf0, lf1, lf2], axis=0))

        def brow(i):
            return b[i:i + 1, :]

        r1 = brow(ref1)
        r2 = jnp.where(t1 < 64, brow(ref2[0]), brow(ref2[1]))
        r3 = jnp.where(t1 < 32, brow(ref3[0]),
                       jnp.where(t1 < 64, brow(ref3[1]),
                                 jnp.where(t1 < 96, brow(ref3[2]), brow(ref3[3]))))
        btot = brow(last)
        d1, d2, d3 = b - r1, b - r2, b - r3
        p1 = lax.dot_general((q * jnp.exp(jnp.minimum(d1, 0.0))).astype(BF16),
                             (k * jnp.exp(jnp.minimum(-d1, 0.0))).astype(BF16),
                             NT_DIMS, preferred_element_type=F32)
        p2 = lax.dot_general((q * jnp.exp(jnp.minimum(d2, 0.0))).astype(BF16),
                             (k * jnp.exp(jnp.minimum(-d2, 0.0))).astype(BF16),
                             NT_DIMS, preferred_element_type=F32)
        p3 = lax.dot_general((q * jnp.exp(d3)).astype(BF16), (k * jnp.exp(-d3)).astype(BF16),
                             NT_DIMS, preferred_element_type=F32)
        att = jnp.where(m1, p1, jnp.where(m2, p2, jnp.where(m3, p3, 0.0)))
        st = st_ref[h]
        o = _bdot(att.astype(BF16), v.astype(BF16))
        o = o + lax.dot_general((q * jnp.exp(b)).astype(BF16), st.astype(BF16),
                                NT_DIMS, preferred_element_type=F32)
        o_ref[0, :, sl] = o
        ks = (k * jnp.exp(btot - b)).astype(BF16)
        st_ref[h] = st * jnp.exp(btot) + _bdot(v.T.astype(BF16), ks)


def _hgrn_scan(proj, lb, reverse, pad):
    B, Tp, _ = proj.shape
    C = HG_CHUNK
    n = Tp // C
    width = HG_HEADS * HG_FDIM

    def spec(colblk):
        if reverse:
            return pl.BlockSpec((1, C, width), lambda b, c: (b, n - 1 - c, colblk))
        return pl.BlockSpec((1, C, width), lambda b, c: (b, c, colblk))

    return pl.pallas_call(
        functools.partial(_hgrn_scan_kernel, reverse=reverse, n_chunks=n, pad=pad),
        grid=(B, n),
        in_specs=[spec(0), spec(1), spec(3 if reverse else 2), _whole((1, width))],
        out_specs=spec(0),
        out_shape=jax.ShapeDtypeStruct((B, Tp, width), F32),
        scratch_shapes=[pltpu.VMEM((HG_HEADS, HG_IDIM, HG_FDIM), F32)],
        compiler_params=_cparams("parallel", "arbitrary"),
        name="hgrn_scan_bwd" if reverse else "hgrn_scan_fwd",
    )(proj, proj, proj, lb)


def _hgrn_out_kernel(of_ref, ob_ref, gate_ref, ng_ref, w_ref, h_ref, g_ref, b_ref, o_ref):
    o = of_ref[0] + ob_ref[0]
    parts = []
    for hd in range(HG_HEADS):
        sl = slice(hd * HG_IDIM, (hd + 1) * HG_IDIM)
        parts.append(_rms_norm(o[:, sl], ng_ref[...]))
    on = jnp.concatenate(parts, axis=1) * _silu(gate_ref[0])
    y = _bdot(on.astype(BF16), w_ref[...])
    o_ref[0] = _layer_norm(DN_ALPHA * h_ref[0] + y, g_ref[...], b_ref[...])


def _hgrn_out(o_f, o_b, proj, norm_g, w_out, h, g, b):
    B, Tp, _ = h.shape
    tm = _row_tile(Tp, 640)
    return pl.pallas_call(
        _hgrn_out_kernel,
        grid=(B, Tp // tm),
        in_specs=[_rows(tm, D_MODEL), _rows(tm, D_MODEL), _rows(tm, D_MODEL, col=4),
                  _whole((1, HG_IDIM)), _whole((D_MODEL, D_MODEL)), _rows(tm, D_MODEL),
                  _whole((1, D_MODEL)), _whole((1, D_MODEL))],
        out_specs=_rows(tm, D_MODEL),
        out_shape=jax.ShapeDtypeStruct((B, Tp, D_MODEL), F32),
        compiler_params=_cparams("parallel", "parallel"),
        name="hgrn_out",
    )(o_f, o_b, proj, norm_g, w_out, h, g, b)


def _hgrn_mixer(h, pad, w_in, w_out, norm_g, lb, g, b):
    proj = _matmul(h, w_in, 640)
    o_f = _hgrn_scan(proj, lb, False, pad)
    o_b = _hgrn_scan(proj, lb, True, pad)
    return _hgrn_out(o_f, o_b, proj, norm_g, w_out, h, g, b)


def _pool_kernel(xp_ref, x_ref, xn_ref, w_ref, sc_ref, g_ref, b_ref, o_ref, xe_ref, *, tt, pad, t_len):
    i = pl.program_id(1)
    H = POOL_HALO
    x = x_ref[0]

    def keep(vals, first_row):
        r = first_row + lax.broadcasted_iota(jnp.int32, (vals.shape[0], 1), 0)
        return jnp.where((r >= pad) & (r < pad + t_len), vals, 0.0)

    xe_ref[0:H, :] = keep(xp_ref[0], i * tt - H)
    xe_ref[H:H + tt, :] = keep(x, i * tt)
    xe_ref[H + tt:H + tt + H, :] = keep(xn_ref[0], i * tt + tt)

    tok = i * tt - pad + lax.broadcasted_iota(jnp.int32, (tt, 1), 0)
    ys = []
    for gi, w in enumerate(POOL_WINDOWS):
        sl = slice(gi * POOL_GROUP, (gi + 1) * POOL_GROUP)
        acc = None
        for j in range(-(w // 2), w // 2):
            piece = xe_ref[H + j:H + j + tt, sl]
            acc = piece if acc is None else acc + piece
        lo = jnp.clip(tok - w // 2, 0, t_len)
        hi = jnp.clip(tok + w // 2, 0, t_len)
        cnt = jnp.maximum(hi - lo, 1).astype(F32)
        p = acc / cnt - x[:, sl]
        ys.append(_bdot(p.astype(BF16), w_ref[gi]))
    y = jnp.concatenate(ys, axis=1) * sc_ref[...]
    o_ref[0] = _layer_norm(DN_ALPHA * x + y, g_ref[...], b_ref[...])


def _pool_mixer(h, pad, w_grp, scale, g, b):
    B, Tp, _ = h.shape
    tt = _row_tile(Tp, 640)
    H = POOL_HALO
    per = tt // H
    last = Tp // H - 1
    return pl.pallas_call(
        functools.partial(_pool_kernel, tt=tt, pad=pad, t_len=Tp - pad),
        grid=(B, Tp // tt),
        in_specs=[pl.BlockSpec((1, H, D_MODEL), lambda b, i: (b, jnp.maximum(i * per - 1, 0), 0)),
                  _rows(tt, D_MODEL),
                  pl.BlockSpec((1, H, D_MODEL), lambda b, i: (b, jnp.minimum((i + 1) * per, last), 0)),
                  _whole((len(POOL_WINDOWS), POOL_GROUP, POOL_GROUP)), _whole((1, D_MODEL)),
                  _whole((1, D_MODEL)), _whole((1, D_MODEL))],
        out_specs=_rows(tt, D_MODEL),
        out_shape=jax.ShapeDtypeStruct((B, Tp, D_MODEL), F32),
        scratch_shapes=[pltpu.VMEM((tt + 2 * H, D_MODEL), F32)],
        compiler_params=_cparams("parallel", "parallel"),
        name="pool_mixer",
    )(h, h, h, w_grp, scale, g, b)


def _rope_tables(tp, pad, d):
    inv = 1.0 / (ROPE_THETA ** (jnp.arange(0, d, 2, dtype=F32) / d))
    pos = (jnp.arange(tp) - pad).astype(F32)
    ang = pos[:, None] * inv[None, :]
    cos, sin = jnp.cos(ang), jnp.sin(ang)
    reps = LANE // d
    return (jnp.tile(jnp.concatenate([cos, cos], axis=1), (1, reps)),
            jnp.tile(jnp.concatenate([-sin, sin], axis=1), (1, reps)))


def _swa_proj_kernel(x_ref, w_ref, cos_ref, sin_ref, q_ref, k_ref, v_ref):
    qkv = _bdot(x_ref[0].astype(BF16), w_ref[...])
    cos, sin = cos_ref[...], sin_ref[...]
    nq = SW_HEADS * SW_HEAD_DIM
    nk = SW_KV_HEADS * SW_HEAD_DIM

    def rope(xh):
        return xh * cos + pltpu.roll(xh, SW_HEAD_DIM // 2, 1) * sin

    for hd in range(SW_HEADS):
        sl = slice(hd * SW_HEAD_DIM, (hd + 1) * SW_HEAD_DIM)
        q_ref[0, :, sl] = rope(qkv[:, sl]).astype(BF16)
    for hd in range(SW_KV_HEADS):
        sl = slice(hd * SW_HEAD_DIM, (hd + 1) * SW_HEAD_DIM)
        k_ref[0, :, sl] = rope(qkv[:, nq + hd * SW_HEAD_DIM:nq + (hd + 1) * SW_HEAD_DIM]).astype(BF16)
    v_ref[0] = qkv[:, nq + nk:].astype(BF16)


def _swa_attn_kernel(sink_ref, q_ref, kp_ref, kc_ref, kn_ref, vp_ref, vc_ref, vn_ref,
                     w_ref, h_ref, g_ref, b_ref, o_ref, *, pad, tp):
    i = pl.program_id(1)
    blk = SW_WINDOW
    q = q_ref[0]
    kcat = jnp.concatenate([kp_ref[0], kc_ref[0], kn_ref[0]], axis=0)
    vcat = jnp.concatenate([vp_ref[0], vc_ref[0], vn_ref[0]], axis=0)
    rq = i * blk + lax.broadcasted_iota(jnp.int32, (blk, 3 * blk), 0)
    rk = (i - 1) * blk + lax.broadcasted_iota(jnp.int32, (blk, 3 * blk), 1)
    ok = (rk >= pad) & (rk < tp) & (jnp.abs(rq - rk) <= SW_WINDOW)
    scale = SW_HEAD_DIM ** -0.5
    outs = []
    for hd in range(SW_HEADS):
        kv = hd // SW_GROUPS
        ksl = slice(kv * SW_HEAD_DIM, (kv + 1) * SW_HEAD_DIM)
        s = lax.dot_general(q[:, hd * SW_HEAD_DIM:(hd + 1) * SW_HEAD_DIM], kcat[:, ksl],
                            NT_DIMS, preferred_element_type=F32) * scale
        s = jnp.where(ok, s, -jnp.inf)
        sk = sink_ref[hd]
        m = jnp.maximum(jnp.max(s, axis=-1, keepdims=True), sk)
        p = jnp.exp(s - m)
        den = jnp.sum(p, axis=-1, keepdims=True) + jnp.exp(sk - m)
        outs.append(_bdot((p * (1.0 / den)).astype(BF16), vcat[:, ksl]))
    o = jnp.concatenate(outs, axis=1).astype(BF16)
    y = _bdot(o, w_ref[...])
    o_ref[0] = _layer_norm(DN_ALPHA * h_ref[0] + y, g_ref[...], b_ref[...])


def _swa_mixer(h, pad, w_qkv, w_out, sink, g, b):
    B, Tp, _ = h.shape
    cos, sin = _rope_tables(Tp, pad, SW_HEAD_DIM)
    tm = _row_tile(Tp, 640)
    nq = SW_HEADS * SW_HEAD_DIM
    nk = SW_KV_HEADS * SW_HEAD_DIM
    tab = pl.BlockSpec((tm, LANE), lambda b, i: (i, 0))
    q, k, v = pl.pallas_call(
        _swa_proj_kernel,
        grid=(B, Tp // tm),
        in_specs=[_rows(tm, D_MODEL), _whole((D_MODEL, nq + 2 * nk)), tab, tab],
        out_specs=[_rows(tm, nq), _rows(tm, nk), _rows(tm, nk)],
        out_shape=[jax.ShapeDtypeStruct((B, Tp, nq), BF16), jax.ShapeDtypeStruct((B, Tp, nk), BF16),
                   jax.ShapeDtypeStruct((B, Tp, nk), BF16)],
        compiler_params=_cparams("parallel", "parallel"),
        name="swa_proj",
    )(h, w_qkv, cos, sin)

    blk = SW_WINDOW
    nb = Tp // blk
    prev = pl.BlockSpec((1, blk, nk), lambda b, i: (b, jnp.maximum(i - 1, 0), 0))
    cur = pl.BlockSpec((1, blk, nk), lambda b, i: (b, i, 0))
    nxt = pl.BlockSpec((1, blk, nk), lambda b, i: (b, jnp.minimum(i + 1, nb - 1), 0))
    return pl.pallas_call(
        functools.partial(_swa_attn_kernel, pad=pad, tp=Tp),
        grid=(B, nb),
        in_specs=[pl.BlockSpec(memory_space=pltpu.SMEM),
                  _rows(blk, nq), prev, cur, nxt, prev, cur, nxt,
                  _whole((nq, D_MODEL)), _rows(blk, D_MODEL),
                  _whole((1, D_MODEL)), _whole((1, D_MODEL))],
        out_specs=_rows(blk, D_MODEL),
        out_shape=jax.ShapeDtypeStruct((B, Tp, D_MODEL), F32),
        compiler_params=_cparams("parallel", "parallel"),
        name="swa_attn",
    )(sink, q, k, k, k, v, v, v, w_out, h, g, b)


def _mla_proj_kernel(x_ref, wdq_ref, gq_ref, wuq_ref, wdkv_ref, gkv_ref, wukv_ref, cos_ref, sin_ref,
                     q_ref, k_ref, v_ref, *, pad, tm):
    i = pl.program_id(1)
    xb = x_ref[0].astype(BF16)
    cq = _rms_norm(_bdot(xb, wdq_ref[...]), gq_ref[...])
    q = _bdot(cq.astype(BF16), wuq_ref[...])
    ckv = _bdot(xb, wdkv_ref[...])
    c = _rms_norm(ckv[:, :MLA_KV_RANK], gkv_ref[...])
    kv = _bdot(c.astype(BF16), wukv_ref[...])
    cos, sin = cos_ref[...], sin_ref[...]
    lane = lax.broadcasted_iota(jnp.int32, (tm, LANE), 1)
    low = (lane % MLA_ROPE) < MLA_ROPE // 2
    spare = lane == MLA_ROPE

    def rope(xr):
        swapped = jnp.where(low, pltpu.roll(xr, LANE - MLA_ROPE // 2, 1), pltpu.roll(xr, MLA_ROPE // 2, 1))
        return xr * cos + swapped * sin

    filler = (i * tm + lax.broadcasted_iota(jnp.int32, (tm, 1), 0)) < pad
    kr = jnp.where(spare, jnp.where(filler, KEY_MASK_BIAS, 0.0), rope(ckv[:, MLA_KV_RANK:])).astype(BF16)
    for hd in range(MLA_HEADS):
        base = hd * MLA_QK
        q_ref[0, :, base:base + MLA_NOPE] = q[:, base:base + MLA_NOPE].astype(BF16)
        qr = jnp.where(spare, 1.0, rope(q[:, base + MLA_NOPE:base + MLA_QK]))
        q_ref[0, :, base + MLA_NOPE:base + MLA_QK] = qr.astype(BF16)
        k_ref[0, :, base:base + MLA_NOPE] = kv[:, hd * MLA_NOPE:(hd + 1) * MLA_NOPE].astype(BF16)
        k_ref[0, :, base + MLA_NOPE:base + MLA_QK] = kr
    v_ref[0] = kv[:, MLA_HEADS * MLA_NOPE:].astype(BF16)


def _mla_attn_kernel(q_ref, k_ref, v_ref, o_ref, s_ref, p_ref, alpha_ref, m_ref, l_ref, acc_ref,
                     *, kc, n_main):
    c = (MLA_NOPE + MLA_ROPE) ** -0.5 * LOG2_E
    n_sub = kc // LANE

    def chunk_rows(t):
        return pl.ds(pl.multiple_of(t * kc, kc), kc)

    def softmax_update(parts):
        m = m_ref[...]
        m_new = jnp.maximum(m, jnp.max(functools.reduce(jnp.maximum, parts), axis=-1, keepdims=True))
        alpha = jnp.exp2((m - m_new) * c)
        ps = [jnp.exp2((sp - m_new) * c) for sp in parts]
        l_ref[...] = alpha * l_ref[...] + functools.reduce(jnp.add, ps)
        m_ref[...] = m_new
        return alpha, [x.astype(BF16) for x in ps]

    def stage_a(t, slot):
        s_ref[slot] = lax.dot_general(q_ref[0], k_ref[0, chunk_rows(t), :], NT_DIMS,
                                      preferred_element_type=F32)

    def stage_b(slot):
        alpha, ps = softmax_update([s_ref[slot, :, i * LANE:(i + 1) * LANE] for i in range(n_sub)])
        alpha_ref[slot] = alpha
        for i in range(n_sub):
            p_ref[slot, :, i * LANE:(i + 1) * LANE] = ps[i]

    def stage_c(t, slot):
        acc_ref[...] = alpha_ref[slot] * acc_ref[...] + _bdot(p_ref[slot], v_ref[0, chunk_rows(t), :])

    def tick(t, parity, a=True, b=True, cc=True):
        if a:
            stage_a(t, parity)
        if b:
            stage_b(1 - parity)
        if cc:
            stage_c(t - 2, parity)

    def static_tick(t):
        tick(t, t % 2, a=t < n_main, b=1 <= t <= n_main, cc=2 <= t <= n_main + 1)

    m_ref[...] = jnp.full(m_ref.shape, -jnp.inf, F32)
    l_ref[...] = jnp.zeros(l_ref.shape, F32)
    tail = slice(n_main * kc, None)
    s_tail = lax.dot_general(q_ref[0], k_ref[0, tail, :], NT_DIMS, preferred_element_type=F32)
    _, ps = softmax_update([s_tail[:, i * LANE:(i + 1) * LANE] for i in range(s_tail.shape[1] // LANE)])
    acc_ref[...] = _bdot(jnp.concatenate(ps, axis=1), v_ref[0, tail, :])

    static_tick(0)
    static_tick(1)
    start = 2
    if (n_main - start) % 2 == 1:
        static_tick(2)
        start = 3
    if n_main > start:
        def pair(i, carry):
            t0 = start + 2 * i
            tick(t0, start % 2)
            tick(t0 + 1, (start + 1) % 2)
            return carry
        lax.fori_loop(0, (n_main - start) // 2, pair, 0)
    for t in (n_main, n_main + 1):
        if t >= start:
            static_tick(t)

    inv = 1.0 / jnp.sum(l_ref[...], axis=-1, keepdims=True)
    o_ref[0] = (acc_ref[...] * inv).astype(BF16)


def _mla_mixer(h, pad, w_dq, q_norm_g, w_uq, w_dkv, kv_norm_g, w_ukv, w_out, g, b):
    B, Tp, _ = h.shape
    cos, sin = _rope_tables(Tp, pad, MLA_ROPE)
    tm = LANE
    nb = Tp // tm
    nqk = MLA_HEADS * MLA_QK
    nv = MLA_HEADS * MLA_V
    tab = pl.BlockSpec((tm, LANE), lambda b, i: (i, 0))

    def rotated(width):
        return pl.BlockSpec((1, tm, width), lambda b, i: (b, (i + nb - 1) % nb, 0))

    q, k, v = pl.pallas_call(
        functools.partial(_mla_proj_kernel, pad=pad, tm=tm),
        grid=(B, nb),
        in_specs=[_rows(tm, D_MODEL), _whole(w_dq.shape), _whole((1, MLA_Q_RANK)), _whole(w_uq.shape),
                  _whole(w_dkv.shape), _whole((1, MLA_KV_RANK)), _whole(w_ukv.shape), tab, tab],
        out_specs=[_rows(tm, nqk), rotated(nqk), rotated(nv)],
        out_shape=[jax.ShapeDtypeStruct((B, Tp, nqk), BF16), jax.ShapeDtypeStruct((B, Tp, nqk), BF16),
                   jax.ShapeDtypeStruct((B, Tp, nv), BF16)],
        compiler_params=_cparams("parallel", "parallel"),
        name="mla_proj",
    )(h, w_dq, q_norm_g, w_uq, w_dkv, kv_norm_g, w_ukv, cos, sin)

    tq = _row_tile(Tp, 640)
    kc = _row_tile(Tp - LANE, 512)
    n_main = (Tp - LANE) // kc
    o = pl.pallas_call(
        functools.partial(_mla_attn_kernel, kc=kc, n_main=n_main),
        grid=(B, MLA_HEADS, Tp // tq),
        in_specs=[pl.BlockSpec((1, tq, MLA_QK), lambda b, hd, i: (b, i, hd)),
                  pl.BlockSpec((1, Tp, MLA_QK), lambda b, hd, i: (b, 0, hd)),
                  pl.BlockSpec((1, Tp, MLA_V), lambda b, hd, i: (b, 0, hd))],
        out_specs=pl.BlockSpec((1, tq, MLA_V), lambda b, hd, i: (b, i, hd)),
        out_shape=jax.ShapeDtypeStruct((B, Tp, nv), BF16),
        scratch_shapes=[pltpu.VMEM((2, tq, kc), F32), pltpu.VMEM((2, tq, kc), BF16),
                        pltpu.VMEM((2, tq, LANE), F32), pltpu.VMEM((tq, LANE), F32),
                        pltpu.VMEM((tq, LANE), F32), pltpu.VMEM((tq, MLA_V), F32)],
        compiler_params=_cparams("parallel", "parallel", "parallel"),
        name="mla_attn",
    )(q, k, v)
    return _proj_ln(o, w_out, h, g, b)


def _prepare_params(p):
    bf = lambda a: a.astype(BF16)
    row = lambda a: a.astype(F32).reshape(1, -1)
    out = dict(p)
    for name in ('a_w_in', 'a_w_out', 'b_w_grp', 'c_w_qkv', 'c_w_out', 'd_w_dq', 'd_w_out',
                 'ffn_w_gu', 'ffn_w_down'):
        out[name] = bf(p[name])
    n_d = p['d_w_uq'].shape[0]
    wuq = p['d_w_uq'].reshape(n_d, MLA_Q_RANK, MLA_HEADS, MLA_NOPE + MLA_ROPE)
    wuq = jnp.pad(wuq, ((0, 0), (0, 0), (0, 0), (0, MLA_QK - MLA_NOPE - MLA_ROPE)))
    out['d_w_uq'] = bf(wuq.reshape(n_d, MLA_Q_RANK, MLA_HEADS * MLA_QK))
    out['d_w_dkv'] = bf(jnp.pad(p['d_w_dkv'], ((0, 0), (0, 0), (0, LANE - MLA_ROPE))))
    wukv = p['d_w_ukv'].reshape(n_d, MLA_KV_RANK, MLA_HEADS, MLA_NOPE + MLA_V)
    out['d_w_ukv'] = bf(jnp.concatenate(
        [wukv[..., :MLA_NOPE].reshape(n_d, MLA_KV_RANK, MLA_HEADS * MLA_NOPE),
         wukv[..., MLA_NOPE:].reshape(n_d, MLA_KV_RANK, MLA_HEADS * MLA_V)], axis=-1))
    out['lb_all'] = jnp.cumsum(jax.nn.softmax(p['hg_lb_logits'].astype(F32), axis=0), axis=0)
    return out


def _trunk(x, p):
    B, S, _ = x.shape
    T = N_META + S
    Tp = -(-T // LANE) * LANE
    pad = Tp - T
    meta = jnp.broadcast_to(p['meta_tokens'].astype(x.dtype)[None], (B, N_META, D_MODEL))
    h = jnp.concatenate([jnp.zeros((B, pad, D_MODEL), x.dtype), meta, x], axis=1)
    row = lambda a: a.astype(F32).reshape(1, -1)
    for i in range(DEPTH):
        kind, j = i % N_MIXERS, i // N_MIXERS
        g0, b0 = row(p['ln_g'][i, 0]), row(p['ln_b'][i, 0])
        if kind == 0:
            h = _hgrn_mixer(h, pad, p['a_w_in'][j], p['a_w_out'][j], row(p['a_norm_g'][j]),
                            row(p['lb_all'][i]), g0, b0)
        elif kind == 1:
            h = _pool_mixer(h, pad, p['b_w_grp'][j], row(p['b_scale'][j]), g0, b0)
        elif kind == 2:
            h = _swa_mixer(h, pad, p['c_w_qkv'][j], p['c_w_out'][j], p['c_sink'][j].astype(F32), g0, b0)
        else:
            h = _mla_mixer(h, pad, p['d_w_dq'][j], row(p['d_q_norm_g'][j]), p['d_w_uq'][j],
                           p['d_w_dkv'][j], row(p['d_kv_norm_g'][j]), p['d_w_ukv'][j],
                           p['d_w_out'][j], g0, b0)
        h = _ffn(h, p['ffn_w_gu'][i], p['ffn_w_down'][i], row(p['ln_g'][i, 1]), row(p['ln_b'][i, 1]))
    return h[:, pad + N_META:]


def kernel(x_prompt, x_sample, meta_tokens, hg_lb_logits, a_w_in, a_w_out, a_norm_g, b_w_grp, b_scale,
           c_w_qkv, c_w_out, c_sink, d_w_dq, d_q_norm_g, d_w_uq, d_w_dkv, d_kv_norm_g, d_w_ukv, d_w_out,
           ffn_w_gu, ffn_w_down, ln_g, ln_b):
    params = _prepare_params({
        'meta_tokens': meta_tokens, 'hg_lb_logits': hg_lb_logits,
        'a_w_in': a_w_in, 'a_w_out': a_w_out, 'a_norm_g': a_norm_g,
        'b_w_grp': b_w_grp, 'b_scale': b_scale,
        'c_w_qkv': c_w_qkv, 'c_w_out': c_w_out, 'c_sink': c_sink,
        'd_w_dq': d_w_dq, 'd_q_norm_g': d_q_norm_g, 'd_w_uq': d_w_uq, 'd_w_dkv': d_w_dkv,
        'd_kv_norm_g': d_kv_norm_g, 'd_w_ukv': d_w_ukv, 'd_w_out': d_w_out,
        'ffn_w_gu': ffn_w_gu, 'ffn_w_down': ffn_w_down, 'ln_g': ln_g, 'ln_b': ln_b,
    })
    return (_trunk(x_prompt, params), _trunk(x_sample, params))
```

```python
import functools

import jax
import jax.numpy as jnp
from jax import lax
from jax.experimental import pallas as pl
from jax.experimental.pallas import tpu as pltpu

F32 = jnp.float32
BF16 = jnp.bfloat16

D_MODEL = 1024
DEPTH = 4
N_META = 16
N_MIXERS = 4
DN_ALPHA = (2 * DEPTH) ** 0.25
LN_EPS = 1e-5
RMS_EPS = 1e-6
ROPE_THETA = 10000.0

HG_HEADS = 8
HG_FDIM = 128
HG_IDIM = D_MODEL // HG_HEADS
POOL_WINDOWS = (2, 4, 8, 16)
POOL_GROUP = D_MODEL // len(POOL_WINDOWS)
POOL_HALO = 8
SW_HEADS = 8
SW_KV_HEADS = 2
SW_GROUPS = SW_HEADS // SW_KV_HEADS
SW_HEAD_DIM = D_MODEL // SW_HEADS
SW_WINDOW = 128
MLA_HEADS = 16
MLA_Q_RANK = 256
MLA_KV_RANK = 256
MLA_NOPE = 128
MLA_ROPE = 64
MLA_V = 128
MLA_QK = 256
MLA_Q_TILE_MAX = 640
MLA_KEY_CHUNK_MAX = 1536
D_FF = ((8 * D_MODEL + 3 * 256 - 1) // (3 * 256)) * 256

LANE = 128
HG_CHUNK = 128
VMEM_LIMIT_BYTES = 56 * 1024 * 1024
KEY_MASK_BIAS = -1e30
LOG2_E = 1.4426950408889634

NT_DIMS = (((1,), (1,)), ((), ()))


def _cparams(*sem):
    return pltpu.CompilerParams(dimension_semantics=sem, vmem_limit_bytes=VMEM_LIMIT_BYTES)


def _row_tile(tp, max_rows):
    best = LANE
    for t in range(LANE, max_rows + 1, LANE):
        if tp % t == 0:
            best = t
    return best


def _whole(shape):
    nd = len(shape)
    return pl.BlockSpec(shape, lambda *_: (0,) * nd, pipeline_mode=pl.Buffered(1))


def _rows(tm, width, col=0):
    return pl.BlockSpec((1, tm, width), lambda b, i: (b, i, col))


def _layer_norm(z, g, b):
    mu = jnp.mean(z, axis=-1, keepdims=True)
    zc = z - mu
    var = jnp.mean(zc * zc, axis=-1, keepdims=True)
    return zc * lax.rsqrt(var + LN_EPS) * g + b


def _rms_norm(x, g):
    return x * lax.rsqrt(jnp.mean(x * x, axis=-1, keepdims=True) + RMS_EPS) * g


def _sigmoid(x):
    return 1.0 / (1.0 + jnp.exp(-x))


def _silu(x):
    return x * _sigmoid(x)


def _bdot(a, b):
    return jnp.dot(a, b, preferred_element_type=F32)


def _matmul_kernel(x_ref, w_ref, o_ref):
    o_ref[0] = _bdot(x_ref[0].astype(BF16), w_ref[...])


def _matmul(x, w, max_rows):
    B, Tp, K = x.shape
    N = w.shape[1]
    tm = _row_tile(Tp, max_rows)
    return pl.pallas_call(
        _matmul_kernel,
        grid=(B, Tp // tm),
        in_specs=[_rows(tm, K), _whole((K, N))],
        out_specs=_rows(tm, N),
        out_shape=jax.ShapeDtypeStruct((B, Tp, N), F32),
        compiler_params=_cparams("parallel", "parallel"),
        name="in_proj",
    )(x, w)


def _proj_ln_kernel(a_ref, w_ref, h_ref, g_ref, b_ref, o_ref):
    y = _bdot(a_ref[0], w_ref[...])
    o_ref[0] = _layer_norm(DN_ALPHA * h_ref[0] + y, g_ref[...], b_ref[...])


def _proj_ln(a, w, h, g, b, max_rows=640):
    B, Tp, K = a.shape
    tm = _row_tile(Tp, max_rows)
    return pl.pallas_call(
        _proj_ln_kernel,
        grid=(B, Tp // tm),
        in_specs=[_rows(tm, K), _whole((K, D_MODEL)), _rows(tm, D_MODEL),
                  _whole((1, D_MODEL)), _whole((1, D_MODEL))],
        out_specs=_rows(tm, D_MODEL),
        out_shape=jax.ShapeDtypeStruct((B, Tp, D_MODEL), F32),
        compiler_params=_cparams("parallel", "parallel"),
        name="out_proj_ln",
    )(a, w, h, g, b)


FFN_CHUNKS = 2


def _ffn_kernel(x_ref, wgu_ref, wd_ref, g_ref, b_ref, o_ref):
    x = x_ref[0]
    xb = x.astype(BF16)
    ch = D_FF // FFN_CHUNKS
    y = None
    for c in range(FFN_CHUNKS):
        gate = _bdot(xb, wgu_ref[:, c * ch:(c + 1) * ch])
        up = _bdot(xb, wgu_ref[:, D_FF + c * ch:D_FF + (c + 1) * ch])
        act = (_silu(gate) * up).astype(BF16)
        part = _bdot(act, wd_ref[c * ch:(c + 1) * ch, :])
        y = part if y is None else y + part
    o_ref[0] = _layer_norm(DN_ALPHA * x + y, g_ref[...], b_ref[...])


def _ffn(h, wgu, wd, g, b):
    B, Tp, _ = h.shape
    tm = _row_tile(Tp, 640)
    return pl.pallas_call(
        _ffn_kernel,
        grid=(B, Tp // tm),
        in_specs=[_rows(tm, D_MODEL), _whole((D_MODEL, 2 * D_FF)), _whole((D_FF, D_MODEL)),
                  _whole((1, D_MODEL)), _whole((1, D_MODEL))],
        out_specs=_rows(tm, D_MODEL),
        out_shape=jax.ShapeDtypeStruct((B, Tp, D_MODEL), F32),
        compiler_params=_cparams("parallel", "parallel"),
        name="ffn",
    )(h, wgu, wd, g, b)


def _hgrn_scan_kernel(q_ref, v_ref, a_ref, lb_ref, o_ref, st_ref, *, reverse, n_chunks, pad):
    C = HG_CHUNK
    c = pl.program_id(1)

    @pl.when(c == 0)
    def _():
        st_ref[...] = jnp.zeros_like(st_ref)

    chunk = (n_chunks - 1 - c) if reverse else c
    row = lax.broadcasted_iota(jnp.int32, (C, C), 0)
    col = lax.broadcasted_iota(jnp.int32, (C, C), 1)
    t1 = lax.broadcasted_iota(jnp.int32, (C, 1), 0)
    rb, cb = row // 32, col // 32
    if reverse:
        cum = col >= row
        m1 = (row < 64) & (col >= 64)
        m2 = (rb // 2 == cb // 2) & (rb % 2 == 0) & (cb % 2 == 1)
        m3 = (rb == cb) & (col >= row)
        ref1, ref2, ref3, last = 64, (32, 96), (16, 48, 80, 112), 0
    else:
        cum = col <= row
        m1 = (row >= 64) & (col < 64)
        m2 = (rb // 2 == cb // 2) & (rb % 2 == 1) & (cb % 2 == 0)
        m3 = (rb == cb) & (col <= row)
        ref1, ref2, ref3, last = 63, (31, 95), (15, 47, 79, 111), C - 1
    cum = cum.astype(BF16)
    cum3 = jnp.concatenate([cum, cum, cum], axis=1)
    valid = (chunk * C + t1) >= pad

    for h in range(HG_HEADS):
        sl = slice(h * HG_FDIM, (h + 1) * HG_FDIM)
        q = _silu(q_ref[0, :, sl])
        v = v_ref[0, :, sl]
        a = a_ref[0, :, sl]
        lb = lb_ref[:, sl]
        f = lb + (1.0 - lb) * _sigmoid(a)
        k = jnp.where(valid, (1.0 - lb) * _sigmoid(-a), 0.0)
        lf = jnp.where(valid, jnp.log(f), 0.0)
        lf0 = lf.astype(BF16)
        rem = lf - lf0.astype(F32)
        lf1 = rem.astype(BF16)
        lf2 = (rem - lf1.astype(F32)).astype(BF16)
        b = _bdot(cum3, jnp.concatenate([lf0, lf1, lf2], axis=0))

        def brow(i):
            return b[i:i + 1, :]

        r1 = brow(ref1)
        r2 = jnp.where(t1 < 64, brow(ref2[0]), brow(ref2[1]))
        r3 = jnp.where(t1 < 32, brow(ref3[0]),
                       jnp.where(t1 < 64, brow(ref3[1]),
                                 jnp.where(t1 < 96, brow(ref3[2]), brow(ref3[3]))))
        btot = brow(last)
        d1, d2, d3 = b - r1, b - r2, b - r3
        p1 = lax.dot_general((q * jnp.exp(jnp.minimum(d1, 0.0))).astype(BF16),
                             (k * jnp.exp(jnp.minimum(-d1, 0.0))).astype(BF16),
                             NT_DIMS, preferred_element_type=F32)
        p2 = lax.dot_general((q * jnp.exp(jnp.minimum(d2, 0.0))).astype(BF16),
                             (k * jnp.exp(jnp.minimum(-d2, 0.0))).astype(BF16),
                             NT_DIMS, preferred_element_type=F32)
        p3 = lax.dot_general((q * jnp.exp(d3)).astype(BF16), (k * jnp.exp(-d3)).astype(BF16),
                             NT_DIMS, preferred_element_type=F32)
        att = jnp.where(m1, p1, jnp.where(m2, p2, jnp.where(m3, p3, 0.0)))
        st = st_ref[h]
        o = _bdot(att.astype(BF16), v.astype(BF16))
        o = o + lax.dot_general((q * jnp.exp(b)).astype(BF16), st.astype(BF16),
                                NT_DIMS, preferred_element_type=F32)
        o_ref[0, :, sl] = o
        ks = (k * jnp.exp(btot - b)).astype(BF16)
        st_ref[h] = st * jnp.exp(btot) + _bdot(v.T.astype(BF16), ks)


def _hgrn_scan(proj, lb, reverse, pad):
    B, Tp, _ = proj.shape
    C = HG_CHUNK
    n = Tp // C
    width = HG_HEADS * HG_FDIM

    def spec(colblk):
        if reverse:
            return pl.BlockSpec((1, C, width), lambda b, c: (b, n - 1 - c, colblk))
        return pl.BlockSpec((1, C, width), lambda b, c: (b, c, colblk))

    return pl.pallas_call(
        functools.partial(_hgrn_scan_kernel, reverse=reverse, n_chunks=n, pad=pad),
        grid=(B, n),
        in_specs=[spec(0), spec(1), spec(3 if reverse else 2), _whole((1, width))],
        out_specs=spec(0),
        out_shape=jax.ShapeDtypeStruct((B, Tp, width), F32),
        scratch_shapes=[pltpu.VMEM((HG_HEADS, HG_IDIM, HG_FDIM), F32)],
        compiler_params=_cparams("parallel", "arbitrary"),
        name="hgrn_scan_bwd" if reverse else "hgrn_scan_fwd",
    )(proj, proj, proj, lb)


def _hgrn_out_kernel(of_ref, ob_ref, gate_ref, ng_ref, w_ref, h_ref, g_ref, b_ref, o_ref):
    o = of_ref[0] + ob_ref[0]
    parts = []
    for hd in range(HG_HEADS):
        sl = slice(hd * HG_IDIM, (hd + 1) * HG_IDIM)
        parts.append(_rms_norm(o[:, sl], ng_ref[...]))
    on = jnp.concatenate(parts, axis=1) * _silu(gate_ref[0])
    y = _bdot(on.astype(BF16), w_ref[...])
    o_ref[0] = _layer_norm(DN_ALPHA * h_ref[0] + y, g_ref[...], b_ref[...])


def _hgrn_out(o_f, o_b, proj, norm_g, w_out, h, g, b):
    B, Tp, _ = h.shape
    tm = _row_tile(Tp, 640)
    return pl.pallas_call(
        _hgrn_out_kernel,
        grid=(B, Tp // tm),
        in_specs=[_rows(tm, D_MODEL), _rows(tm, D_MODEL), _rows(tm, D_MODEL, col=4),
                  _whole((1, HG_IDIM)), _whole((D_MODEL, D_MODEL)), _rows(tm, D_MODEL),
                  _whole((1, D_MODEL)), _whole((1, D_MODEL))],
        out_specs=_rows(tm, D_MODEL),
        out_shape=jax.ShapeDtypeStruct((B, Tp, D_MODEL), F32),
        compiler_params=_cparams("parallel", "parallel"),
        name="hgrn_out",
    )(o_f, o_b, proj, norm_g, w_out, h, g, b)


def _hgrn_mixer(h, pad, w_in, w_out, norm_g, lb, g, b):
    proj = _matmul(h, w_in, 640)
    o_f = _hgrn_scan(proj, lb, False, pad)
    o_b = _hgrn_scan(proj, lb, True, pad)
    return _hgrn_out(o_f, o_b, proj, norm_g, w_out, h, g, b)


def _pool_kernel(xp_ref, x_ref, xn_ref, w_ref, sc_ref, g_ref, b_ref, o_ref, xe_ref, *, tt, pad, t_len):
    i = pl.program_id(1)
    H = POOL_HALO
    x = x_ref[0]

    def keep(vals, first_row):
        r = first_row + lax.broadcasted_iota(jnp.int32, (vals.shape[0], 1), 0)
        return jnp.where((r >= pad) & (r < pad + t_len), vals, 0.0)

    xe_ref[0:H, :] = keep(xp_ref[0], i * tt - H)
    xe_ref[H:H + tt, :] = keep(x, i * tt)
    xe_ref[H + tt:H + tt + H, :] = keep(xn_ref[0], i * tt + tt)

    tok = i * tt - pad + lax.broadcasted_iota(jnp.int32, (tt, 1), 0)
    ys = []
    for gi, w in enumerate(POOL_WINDOWS):
        sl = slice(gi * POOL_GROUP, (gi + 1) * POOL_GROUP)
        acc = None
        for j in range(-(w // 2), w // 2):
            piece = xe_ref[H + j:H + j + tt, sl]
            acc = piece if acc is None else acc + piece
        lo = jnp.clip(tok - w // 2, 0, t_len)
        hi = jnp.clip(tok + w // 2, 0, t_len)
        cnt = jnp.maximum(hi - lo, 1).astype(F32)
        p = acc / cnt - x[:, sl]
        ys.append(_bdot(p.astype(BF16), w_ref[gi]))
    y = jnp.concatenate(ys, axis=1) * sc_ref[...]
    o_ref[0] = _layer_norm(DN_ALPHA * x + y, g_ref[...], b_ref[...])


def _pool_mixer(h, pad, w_grp, scale, g, b):
    B, Tp, _ = h.shape
    tt = _row_tile(Tp, 640)
    H = POOL_HALO
    per = tt // H
    last = Tp // H - 1
    return pl.pallas_call(
        functools.partial(_pool_kernel, tt=tt, pad=pad, t_len=Tp - pad),
        grid=(B, Tp // tt),
        in_specs=[pl.BlockSpec((1, H, D_MODEL), lambda b, i: (b, jnp.maximum(i * per - 1, 0), 0)),
                  _rows(tt, D_MODEL),
                  pl.BlockSpec((1, H, D_MODEL), lambda b, i: (b, jnp.minimum((i + 1) * per, last), 0)),
                  _whole((len(POOL_WINDOWS), POOL_GROUP, POOL_GROUP)), _whole((1, D_MODEL)),
                  _whole((1, D_MODEL)), _whole((1, D_MODEL))],
        out_specs=_rows(tt, D_MODEL),
        out_shape=jax.ShapeDtypeStruct((B, Tp, D_MODEL), F32),
        scratch_shapes=[pltpu.VMEM((tt + 2 * H, D_MODEL), F32)],
        compiler_params=_cparams("parallel", "parallel"),
        name="pool_mixer",
    )(h, h, h, w_grp, scale, g, b)


def _rope_tables(tp, pad, d):
    inv = 1.0 / (ROPE_THETA ** (jnp.arange(0, d, 2, dtype=F32) / d))
    pos = (jnp.arange(tp) - pad).astype(F32)
    ang = pos[:, None] * inv[None, :]
    cos, sin = jnp.cos(ang), jnp.sin(ang)
    reps = LANE // d
    return (jnp.tile(jnp.concatenate([cos, cos], axis=1), (1, reps)),
            jnp.tile(jnp.concatenate([-sin, sin], axis=1), (1, reps)))


def _swa_proj_kernel(x_ref, w_ref, cos_ref, sin_ref, q_ref, k_ref, v_ref):
    qkv = _bdot(x_ref[0].astype(BF16), w_ref[...])
    cos, sin = cos_ref[...], sin_ref[...]
    nq = SW_HEADS * SW_HEAD_DIM
    nk = SW_KV_HEADS * SW_HEAD_DIM

    def rope(xh):
        return xh * cos + pltpu.roll(xh, SW_HEAD_DIM // 2, 1) * sin

    for hd in range(SW_HEADS):
        sl = slice(hd * SW_HEAD_DIM, (hd + 1) * SW_HEAD_DIM)
        q_ref[0, :, sl] = rope(qkv[:, sl]).astype(BF16)
    for hd in range(SW_KV_HEADS):
        sl = slice(hd * SW_HEAD_DIM, (hd + 1) * SW_HEAD_DIM)
        k_ref[0, :, sl] = rope(qkv[:, nq + hd * SW_HEAD_DIM:nq + (hd + 1) * SW_HEAD_DIM]).astype(BF16)
    v_ref[0] = qkv[:, nq + nk:].astype(BF16)


def _swa_attn_kernel(sink_ref, q_ref, kp_ref, kc_ref, kn_ref, vp_ref, vc_ref, vn_ref,
                     w_ref, h_ref, g_ref, b_ref, o_ref, kcat_ref, vcat_ref, att_ref, *, pad, tp, tt):
    i = pl.program_id(1)
    blk = SW_WINDOW
    kcat_ref[0:blk, :] = kp_ref[0]
    kcat_ref[blk:blk + tt, :] = kc_ref[0]
    kcat_ref[blk + tt:, :] = kn_ref[0]
    vcat_ref[0:blk, :] = vp_ref[0]
    vcat_ref[blk:blk + tt, :] = vc_ref[0]
    vcat_ref[blk + tt:, :] = vn_ref[0]
    scale = SW_HEAD_DIM ** -0.5
    dq = lax.broadcasted_iota(jnp.int32, (blk, 3 * blk), 0)
    dk = lax.broadcasted_iota(jnp.int32, (blk, 3 * blk), 1) - blk

    def one_block(j, carry):
        r0 = pl.multiple_of(j * blk, blk)
        rk = i * tt + r0 + dk
        ok = (rk >= pad) & (rk < tp) & (jnp.abs(dq - dk) <= SW_WINDOW)
        ok = jnp.concatenate([ok] * SW_GROUPS, axis=0)
        for kv in range(SW_KV_HEADS):
            heads = range(kv * SW_GROUPS, (kv + 1) * SW_GROUPS)
            ksl = slice(kv * SW_HEAD_DIM, (kv + 1) * SW_HEAD_DIM)
            qs = jnp.concatenate([q_ref[0, pl.ds(r0, blk), hd * SW_HEAD_DIM:(hd + 1) * SW_HEAD_DIM]
                                  for hd in heads], axis=0)
            sk = jnp.concatenate([jnp.full((blk, 1), sink_ref[hd], F32) for hd in heads], axis=0)
            s = lax.dot_general(qs, kcat_ref[pl.ds(r0, 3 * blk), ksl],
                                NT_DIMS, preferred_element_type=F32) * scale
            s = jnp.where(ok, s, -jnp.inf)
            m = jnp.maximum(jnp.max(s, axis=-1, keepdims=True), sk)
            p = jnp.exp(s - m)
            den = jnp.sum(p, axis=-1, keepdims=True) + jnp.exp(sk - m)
            att = _bdot((p * (1.0 / den)).astype(BF16), vcat_ref[pl.ds(r0, 3 * blk), ksl]).astype(BF16)
            for n, hd in enumerate(heads):
                att_ref[pl.ds(r0, blk), hd * SW_HEAD_DIM:(hd + 1) * SW_HEAD_DIM] = att[n * blk:(n + 1) * blk]
        return carry

    lax.fori_loop(0, tt // blk, one_block, 0, unroll=True)
    y = _bdot(att_ref[...], w_ref[...])
    o_ref[0] = _layer_norm(DN_ALPHA * h_ref[0] + y, g_ref[...], b_ref[...])


def _swa_mixer(h, pad, w_qkv, w_out, sink, g, b):
    B, Tp, _ = h.shape
    cos, sin = _rope_tables(Tp, pad, SW_HEAD_DIM)
    tm = _row_tile(Tp, 640)
    nq = SW_HEADS * SW_HEAD_DIM
    nk = SW_KV_HEADS * SW_HEAD_DIM
    tab = pl.BlockSpec((tm, LANE), lambda b, i: (i, 0))
    q, k, v = pl.pallas_call(
        _swa_proj_kernel,
        grid=(B, Tp // tm),
        in_specs=[_rows(tm, D_MODEL), _whole((D_MODEL, nq + 2 * nk)), tab, tab],
        out_specs=[_rows(tm, nq), _rows(tm, nk), _rows(tm, nk)],
        out_shape=[jax.ShapeDtypeStruct((B, Tp, nq), BF16), jax.ShapeDtypeStruct((B, Tp, nk), BF16),
                   jax.ShapeDtypeStruct((B, Tp, nk), BF16)],
        compiler_params=_cparams("parallel", "parallel"),
        name="swa_proj",
    )(h, w_qkv, cos, sin)

    blk = SW_WINDOW
    nb = Tp // blk
    tt = tm
    per = tt // blk
    prev = pl.BlockSpec((1, blk, nk), lambda b, i: (b, jnp.maximum(i * per - 1, 0), 0))
    cur = pl.BlockSpec((1, tt, nk), lambda b, i: (b, i, 0))
    nxt = pl.BlockSpec((1, blk, nk), lambda b, i: (b, jnp.minimum((i + 1) * per, nb - 1), 0))
    return pl.pallas_call(
        functools.partial(_swa_attn_kernel, pad=pad, tp=Tp, tt=tt),
        grid=(B, Tp // tt),
        in_specs=[pl.BlockSpec(memory_space=pltpu.SMEM),
                  _rows(tt, nq), prev, cur, nxt, prev, cur, nxt,
                  _whole((nq, D_MODEL)), _rows(tt, D_MODEL),
                  _whole((1, D_MODEL)), _whole((1, D_MODEL))],
        out_specs=_rows(tt, D_MODEL),
        out_shape=jax.ShapeDtypeStruct((B, Tp, D_MODEL), F32),
        scratch_shapes=[pltpu.VMEM((tt + 2 * blk, nk), BF16), pltpu.VMEM((tt + 2 * blk, nk), BF16),
                        pltpu.VMEM((tt, nq), BF16)],
        compiler_params=_cparams("parallel", "parallel"),
        name="swa_attn",
    )(sink, q, k, k, k, v, v, v, w_out, h, g, b)


def _mla_proj_kernel(x_ref, wdq_ref, gq_ref, wuq_ref, wdkv_ref, gkv_ref, wukv_ref, cos_ref, sin_ref,
                     q_ref, k_ref, v_ref, *, pad, tm):
    i = pl.program_id(1)
    xb = x_ref[0].astype(BF16)
    cq = _rms_norm(_bdot(xb, wdq_ref[...]), gq_ref[...])
    q = _bdot(cq.astype(BF16), wuq_ref[...])
    ckv = _bdot(xb, wdkv_ref[...])
    c = _rms_norm(ckv[:, :MLA_KV_RANK], gkv_ref[...])
    kv = _bdot(c.astype(BF16), wukv_ref[...])
    cos, sin = cos_ref[...], sin_ref[...]
    lane = lax.broadcasted_iota(jnp.int32, (tm, LANE), 1)
    low = (lane % MLA_ROPE) < MLA_ROPE // 2
    spare = lane == MLA_ROPE

    def rope(xr):
        swapped = jnp.where(low, pltpu.roll(xr, LANE - MLA_ROPE // 2, 1), pltpu.roll(xr, MLA_ROPE // 2, 1))
        return xr * cos + swapped * sin

    filler = (i * tm + lax.broadcasted_iota(jnp.int32, (tm, 1), 0)) < pad
    kr = jnp.where(spare, jnp.where(filler, KEY_MASK_BIAS, 0.0), rope(ckv[:, MLA_KV_RANK:])).astype(BF16)
    for hd in range(MLA_HEADS):
        base = hd * MLA_QK
        q_ref[0, :, base:base + MLA_NOPE] = q[:, base:base + MLA_NOPE].astype(BF16)
        qr = jnp.where(spare, 1.0, rope(q[:, base + MLA_NOPE:base + MLA_QK]))
        q_ref[0, :, base + MLA_NOPE:base + MLA_QK] = qr.astype(BF16)
        k_ref[0, :, base:base + MLA_NOPE] = kv[:, hd * MLA_NOPE:(hd + 1) * MLA_NOPE].astype(BF16)
        k_ref[0, :, base + MLA_NOPE:base + MLA_QK] = kr
    v_ref[0] = kv[:, MLA_HEADS * MLA_NOPE:].astype(BF16)


def _mla_attn_kernel(q_ref, k_ref, v_ref, o_ref, s_ref, p_ref, alpha_ref, linv_ref, m_ref, l_ref, acc_ref,
                     *, tq, kc):
    c = (MLA_NOPE + MLA_ROPE) ** -0.5 * LOG2_E
    tp = k_ref.shape[1]
    nc = tp // kc
    n_items = (tp // tq) * nc
    n_sub = kc // LANE

    def tile_chunk(t):
        if isinstance(t, int):
            return divmod(t, nc)
        qi = lax.div(t, jnp.int32(nc))
        return qi, t - qi * nc

    def rows(i, size):
        return pl.ds(i * size, size) if isinstance(i, int) else pl.ds(pl.multiple_of(i * size, size), size)

    def stage_a(u, st, pos):
        qi, j = tile_chunk(u)
        s_ref[st, pos] = lax.dot_general(q_ref[0, rows(qi, tq), :], k_ref[0, rows(j, kc), :], NT_DIMS,
                                         preferred_element_type=F32)

    def stage_b(u, st, pos):
        _, j = tile_chunk(u)
        m = jnp.where(j == 0, -jnp.inf, m_ref[...])
        parts = [s_ref[st, pos, :, i * LANE:(i + 1) * LANE] for i in range(n_sub)]
        m_new = jnp.maximum(m, jnp.max(functools.reduce(jnp.maximum, parts), axis=-1, keepdims=True))
        alpha = jnp.exp2((m - m_new) * c)
        ps = [jnp.exp2((sp - m_new) * c) for sp in parts]
        l = alpha * l_ref[...] + functools.reduce(jnp.add, ps)
        l_ref[...] = l
        m_ref[...] = m_new
        alpha_ref[st, pos] = alpha
        linv_ref[st, pos] = jnp.broadcast_to(1.0 / jnp.sum(l, axis=-1, keepdims=True), l.shape)
        for i in range(n_sub):
            p_ref[st, pos, :, i * LANE:(i + 1) * LANE] = ps[i].astype(BF16)

    def stage_c(u, st, pos):
        qi, j = tile_chunk(u)
        acc = alpha_ref[st, pos] * acc_ref[...] + _bdot(p_ref[st, pos], v_ref[0, rows(j, kc), :])
        acc_ref[...] = acc
        o_ref[0, rows(qi, tq), :] = (acc * linv_ref[st, pos]).astype(BF16)

    def round_(i, par, static):
        for pos in (0, 1):
            ua, ub, uc = 2 * i + pos, 2 * i - 2 + pos, 2 * i - 4 + pos
            if not static or 0 <= ua < n_items:
                stage_a(ua, par, pos)
            if not static or 0 <= ub < n_items:
                stage_b(ub, 1 - par, pos)
            if not static or 0 <= uc < n_items:
                stage_c(uc, par, pos)

    m_ref[...] = jnp.full(m_ref.shape, -jnp.inf, F32)
    l_ref[...] = jnp.zeros(l_ref.shape, F32)
    acc_ref[...] = jnp.zeros(acc_ref.shape, F32)

    first_full, last_full = 2, (n_items - 2) // 2
    last_round = (n_items + 3) // 2
    steady = range(first_full, last_full + 1) if last_full >= first_full else range(0)
    for i in range(last_round + 1):
        if i not in steady:
            if i == last_full + 1 and len(steady) > 0:
                def body(r, carry):
                    @pl.when(r % 2 == 0)
                    def _():
                        round_(r, 0, False)

                    @pl.when(r % 2 == 1)
                    def _():
                        round_(r, 1, False)
                    return carry
                lax.fori_loop(first_full, last_full + 1, body, 0)
            round_(i, i % 2, True)


def _mla_mixer(h, pad, w_dq, q_norm_g, w_uq, w_dkv, kv_norm_g, w_ukv, w_out, g, b):
    B, Tp, _ = h.shape
    cos, sin = _rope_tables(Tp, pad, MLA_ROPE)
    tm = _row_tile(Tp, 384)
    nqk = MLA_HEADS * MLA_QK
    nv = MLA_HEADS * MLA_V
    tab = pl.BlockSpec((tm, LANE), lambda b, i: (i, 0))
    q, k, v = pl.pallas_call(
        functools.partial(_mla_proj_kernel, pad=pad, tm=tm),
        grid=(B, Tp // tm),
        in_specs=[_rows(tm, D_MODEL), _whole(w_dq.shape), _whole((1, MLA_Q_RANK)), _whole(w_uq.shape),
                  _whole(w_dkv.shape), _whole((1, MLA_KV_RANK)), _whole(w_ukv.shape), tab, tab],
        out_specs=[_rows(tm, nqk), _rows(tm, nqk), _rows(tm, nv)],
        out_shape=[jax.ShapeDtypeStruct((B, Tp, nqk), BF16), jax.ShapeDtypeStruct((B, Tp, nqk), BF16),
                   jax.ShapeDtypeStruct((B, Tp, nv), BF16)],
        compiler_params=_cparams("parallel", "parallel"),
        name="mla_proj",
    )(h, w_dq, q_norm_g, w_uq, w_dkv, kv_norm_g, w_ukv, cos, sin)

    tq = _row_tile(Tp, MLA_Q_TILE_MAX)
    kc = _row_tile(Tp, MLA_KEY_CHUNK_MAX)
    o = pl.pallas_call(
        functools.partial(_mla_attn_kernel, tq=tq, kc=kc),
        grid=(B, MLA_HEADS),
        in_specs=[pl.BlockSpec((1, Tp, MLA_QK), lambda b, hd: (b, 0, hd)),
                  pl.BlockSpec((1, Tp, MLA_QK), lambda b, hd: (b, 0, hd)),
                  pl.BlockSpec((1, Tp, MLA_V), lambda b, hd: (b, 0, hd))],
        out_specs=pl.BlockSpec((1, Tp, MLA_V), lambda b, hd: (b, 0, hd)),
        out_shape=jax.ShapeDtypeStruct((B, Tp, nv), BF16),
        scratch_shapes=[pltpu.VMEM((2, 2, tq, kc), F32), pltpu.VMEM((2, 2, tq, kc), BF16),
                        pltpu.VMEM((2, 2, tq, LANE), F32), pltpu.VMEM((2, 2, tq, LANE), F32),
                        pltpu.VMEM((tq, LANE), F32), pltpu.VMEM((tq, LANE), F32),
                        pltpu.VMEM((tq, MLA_V), F32)],
        compiler_params=_cparams("parallel", "parallel"),
        name="mla_attn",
    )(q, k, v)
    return _proj_ln(o, w_out, h, g, b)


def _prepare_params(p):
    bf = lambda a: a.astype(BF16)
    row = lambda a: a.astype(F32).reshape(1, -1)
    out = dict(p)
    for name in ('a_w_in', 'a_w_out', 'b_w_grp', 'c_w_qkv', 'c_w_out', 'd_w_dq', 'd_w_out',
                 'ffn_w_gu', 'ffn_w_down'):
        out[name] = bf(p[name])
    n_d = p['d_w_uq'].shape[0]
    wuq = p['d_w_uq'].reshape(n_d, MLA_Q_RANK, MLA_HEADS, MLA_NOPE + MLA_ROPE)
    wuq = jnp.pad(wuq, ((0, 0), (0, 0), (0, 0), (0, MLA_QK - MLA_NOPE - MLA_ROPE)))
    out['d_w_uq'] = bf(wuq.reshape(n_d, MLA_Q_RANK, MLA_HEADS * MLA_QK))
    out['d_w_dkv'] = bf(jnp.pad(p['d_w_dkv'], ((0, 0), (0, 0), (0, LANE - MLA_ROPE))))
    wukv = p['d_w_ukv'].reshape(n_d, MLA_KV_RANK, MLA_HEADS, MLA_NOPE + MLA_V)
    out['d_w_ukv'] = bf(jnp.concatenate(
        [wukv[..., :MLA_NOPE].reshape(n_d, MLA_KV_RANK, MLA_HEADS * MLA_NOPE),
         wukv[..., MLA_NOPE:].reshape(n_d, MLA_KV_RANK, MLA_HEADS * MLA_V)], axis=-1))
    out['lb_all'] = jnp.cumsum(jax.nn.softmax(p['hg_lb_logits'].astype(F32), axis=0), axis=0)
    return out


def _trunk(x, p):
    B, S, _ = x.shape
    T = N_META + S
    Tp = -(-T // LANE) * LANE
    pad = Tp - T
    meta = jnp.broadcast_to(p['meta_tokens'].astype(x.dtype)[None], (B, N_META, D_MODEL))
    h = jnp.concatenate([jnp.zeros((B, pad, D_MODEL), x.dtype), meta, x], axis=1)
    row = lambda a: a.astype(F32).reshape(1, -1)
    for i in range(DEPTH):
        kind, j = i % N_MIXERS, i // N_MIXERS
        g0, b0 = row(p['ln_g'][i, 0]), row(p['ln_b'][i, 0])
        if kind == 0:
            h = _hgrn_mixer(h, pad, p['a_w_in'][j], p['a_w_out'][j], row(p['a_norm_g'][j]),
                            row(p['lb_all'][i]), g0, b0)
        elif kind == 1:
            h = _pool_mixer(h, pad, p['b_w_grp'][j], row(p['b_scale'][j]), g0, b0)
        elif kind == 2:
            h = _swa_mixer(h, pad, p['c_w_qkv'][j], p['c_w_out'][j], p['c_sink'][j].astype(F32), g0, b0)
        else:
            h = _mla_mixer(h, pad, p['d_w_dq'][j], row(p['d_q_norm_g'][j]), p['d_w_uq'][j],
                           p['d_w_dkv'][j], row(p['d_kv_norm_g'][j]), p['d_w_ukv'][j],
                           p['d_w_out'][j], g0, b0)
        h = _ffn(h, p['ffn_w_gu'][i], p['ffn_w_down'][i], row(p['ln_g'][i, 1]), row(p['ln_b'][i, 1]))
    return h[:, pad + N_META:]


def kernel(x_prompt, x_sample, meta_tokens, hg_lb_logits, a_w_in, a_w_out, a_norm_g, b_w_grp, b_scale,
           c_w_qkv, c_w_out, c_sink, d_w_dq, d_q_norm_g, d_w_uq, d_w_dkv, d_kv_norm_g, d_w_ukv, d_w_out,
           ffn_w_gu, ffn_w_down, ln_g, ln_b):
    params = _prepare_params({
        'meta_tokens': meta_tokens, 'hg_lb_logits': hg_lb_logits,
        'a_w_in': a_w_in, 'a_w_out': a_w_out, 'a_norm_g': a_norm_g,
        'b_w_grp': b_w_grp, 'b_scale': b_scale,
        'c_w_qkv': c_w_qkv, 'c_w_out': c_w_out, 'c_sink': c_sink,
        'd_w_dq': d_w_dq, 'd_q_norm_g': d_q_norm_g, 'd_w_uq': d_w_uq, 'd_w_dkv': d_w_dkv,
        'd_kv_norm_g': d_kv_norm_g, 'd_w_ukv': d_w_ukv, 'd_w_out': d_w_out,
        'ffn_w_gu': ffn_w_gu, 'ffn_w_down': ffn_w_down, 'ln_g': ln_g, 'ln_b': ln_b,
    })
    return (_trunk(x_prompt, params), _trunk(x_sample, params))
```

```python
import functools

import jax
import jax.numpy as jnp
from jax import lax
from jax.experimental import pallas as pl
from jax.experimental.pallas import tpu as pltpu

F32 = jnp.float32
BF16 = jnp.bfloat16

D_MODEL = 1024
DEPTH = 4
N_META = 16
N_MIXERS = 4
DN_ALPHA = (2 * DEPTH) ** 0.25
LN_EPS = 1e-5
RMS_EPS = 1e-6
ROPE_THETA = 10000.0

HG_HEADS = 8
HG_FDIM = 128
HG_IDIM = D_MODEL // HG_HEADS
POOL_WINDOWS = (2, 4, 8, 16)
POOL_GROUP = D_MODEL // len(POOL_WINDOWS)
POOL_HALO = 8
SW_HEADS = 8
SW_KV_HEADS = 2
SW_GROUPS = SW_HEADS // SW_KV_HEADS
SW_HEAD_DIM = D_MODEL // SW_HEADS
SW_WINDOW = 128
MLA_HEADS = 16
MLA_Q_RANK = 256
MLA_KV_RANK = 256
MLA_NOPE = 128
MLA_ROPE = 64
MLA_V = 128
MLA_QK = 256
MLA_Q_TILE_MAX = 640
MLA_KEY_CHUNK_MAX = 1536
D_FF = ((8 * D_MODEL + 3 * 256 - 1) // (3 * 256)) * 256

LANE = 128
HG_CHUNK = 128
VMEM_LIMIT_BYTES = 56 * 1024 * 1024
KEY_MASK_BIAS = -1e30
LOG2_E = 1.4426950408889634

NT_DIMS = (((1,), (1,)), ((), ()))


def _cparams(*sem):
    return pltpu.CompilerParams(dimension_semantics=sem, vmem_limit_bytes=VMEM_LIMIT_BYTES)


def _row_tile(tp, max_rows):
    best = LANE
    for t in range(LANE, max_rows + 1, LANE):
        if tp % t == 0:
            best = t
    return best


def _whole(shape):
    nd = len(shape)
    return pl.BlockSpec(shape, lambda *_: (0,) * nd, pipeline_mode=pl.Buffered(1))


def _rows(tm, width, col=0):
    return pl.BlockSpec((1, tm, width), lambda b, i: (b, i, col))


def _layer_norm(z, g, b):
    mu = jnp.mean(z, axis=-1, keepdims=True)
    zc = z - mu
    var = jnp.mean(zc * zc, axis=-1, keepdims=True)
    return zc * lax.rsqrt(var + LN_EPS) * g + b


def _rms_norm(x, g):
    return x * lax.rsqrt(jnp.mean(x * x, axis=-1, keepdims=True) + RMS_EPS) * g


def _sigmoid(x):
    return 1.0 / (1.0 + jnp.exp(-x))


def _silu_tanh(x):
    hx = 0.5 * x
    return hx * jnp.tanh(hx) + hx


def _silu(x):
    return x * _sigmoid(x)


def _bdot(a, b):
    return jnp.dot(a, b, preferred_element_type=F32)


def _matmul_kernel(x_ref, w_ref, o_ref):
    o_ref[0] = _bdot(x_ref[0].astype(BF16), w_ref[...])


def _matmul(x, w, max_rows):
    B, Tp, K = x.shape
    N = w.shape[1]
    tm = _row_tile(Tp, max_rows)
    return pl.pallas_call(
        _matmul_kernel,
        grid=(B, Tp // tm),
        in_specs=[_rows(tm, K), _whole((K, N))],
        out_specs=_rows(tm, N),
        out_shape=jax.ShapeDtypeStruct((B, Tp, N), F32),
        compiler_params=_cparams("parallel", "parallel"),
        name="in_proj",
    )(x, w)


def _proj_ln_kernel(a_ref, w_ref, h_ref, g_ref, b_ref, o_ref):
    y = _bdot(a_ref[0], w_ref[...])
    o_ref[0] = _layer_norm(DN_ALPHA * h_ref[0] + y, g_ref[...], b_ref[...])


def _proj_ln(a, w, h, g, b, max_rows=640):
    B, Tp, K = a.shape
    tm = _row_tile(Tp, max_rows)
    return pl.pallas_call(
        _proj_ln_kernel,
        grid=(B, Tp // tm),
        in_specs=[_rows(tm, K), _whole((K, D_MODEL)), _rows(tm, D_MODEL),
                  _whole((1, D_MODEL)), _whole((1, D_MODEL))],
        out_specs=_rows(tm, D_MODEL),
        out_shape=jax.ShapeDtypeStruct((B, Tp, D_MODEL), F32),
        compiler_params=_cparams("parallel", "parallel"),
        name="out_proj_ln",
    )(a, w, h, g, b)


FFN_CHUNKS = 2


def _ffn_kernel(x_ref, wgu_ref, wd_ref, g_ref, b_ref, o_ref):
    x = x_ref[...].reshape(x_ref.shape[-2:])
    xb = x.astype(BF16)
    ch = D_FF // FFN_CHUNKS
    y = None
    for c in range(FFN_CHUNKS):
        gate = _bdot(xb, wgu_ref[:, c * ch:(c + 1) * ch])
        up = _bdot(xb, wgu_ref[:, D_FF + c * ch:D_FF + (c + 1) * ch])
        act = (_silu(gate) * up).astype(BF16)
        part = _bdot(act, wd_ref[c * ch:(c + 1) * ch, :])
        y = part if y is None else y + part
    o_ref[0] = _layer_norm(DN_ALPHA * x + y, g_ref[...], b_ref[...])


def _ffn(h, wgu, wd, g, b, skip_rows=0):
    B, Tp, _ = h.shape
    rows_out = Tp - skip_rows
    if skip_rows:
        tm = _row_tile(rows_out, 640)
        h = h.reshape(B * Tp, D_MODEL)
        x_spec = pl.BlockSpec((pl.Element(tm), pl.Element(D_MODEL)),
                              lambda b, i: (pl.multiple_of(b * Tp + skip_rows + i * tm, LANE), 0))
    else:
        tm = _row_tile(Tp, 640)
        x_spec = _rows(tm, D_MODEL)
    return pl.pallas_call(
        _ffn_kernel,
        grid=(B, rows_out // tm),
        in_specs=[x_spec, _whole((D_MODEL, 2 * D_FF)), _whole((D_FF, D_MODEL)),
                  _whole((1, D_MODEL)), _whole((1, D_MODEL))],
        out_specs=_rows(tm, D_MODEL),
        out_shape=jax.ShapeDtypeStruct((B, rows_out, D_MODEL), F32),
        compiler_params=_cparams("parallel", "parallel"),
        name="ffn",
    )(h, wgu, wd, g, b)


def _hgrn_direction_consts(reverse):
    C = HG_CHUNK
    row = lax.broadcasted_iota(jnp.int32, (C, C), 0)
    col = lax.broadcasted_iota(jnp.int32, (C, C), 1)
    rb, cb = row // 32, col // 32
    if reverse:
        cum = col >= row
        m1 = (row < 64) & (col >= 64)
        m2 = (rb // 2 == cb // 2) & (rb % 2 == 0) & (cb % 2 == 1)
        m3 = (rb == cb) & (col >= row)
        refs = (64, (32, 96), (16, 48, 80, 112), 0)
    else:
        cum = col <= row
        m1 = (row >= 64) & (col < 64)
        m2 = (rb // 2 == cb // 2) & (rb % 2 == 1) & (cb % 2 == 0)
        m3 = (rb == cb) & (col <= row)
        refs = (63, (31, 95), (15, 47, 79, 111), C - 1)
    cum = cum.astype(BF16)
    return jnp.concatenate([cum, cum], axis=1), (m1, m2, m3), refs


def _hgrn_head_chunk(q, v, a, lb, valid, st_ref, h, consts):
    cum2, (m1, m2, m3), (ref1, ref2, ref3, last) = consts
    t1 = lax.broadcasted_iota(jnp.int32, (HG_CHUNK, 1), 0)
    c0, c1 = 0.5 * (1.0 + lb), 0.5 * (1.0 - lb)
    ct = c1 * jnp.tanh(0.5 * a)
    f = c0 + ct
    k = jnp.where(valid, c1 - ct, 0.0)
    lf = jnp.where(valid, jnp.log(f) * LOG2_E, 0.0)
    lf0 = lf.astype(BF16)
    lf1 = (lf - lf0.astype(F32)).astype(BF16)
    b = _bdot(cum2, jnp.concatenate([lf0, lf1], axis=0))

    def brow(i):
        return b[i:i + 1, :]

    r1 = brow(ref1)
    r2 = jnp.where(t1 < 64, brow(ref2[0]), brow(ref2[1]))
    r3 = jnp.where(t1 < 32, brow(ref3[0]),
                   jnp.where(t1 < 64, brow(ref3[1]),
                             jnp.where(t1 < 96, brow(ref3[2]), brow(ref3[3]))))
    btot = brow(last)
    d1, d2, d3 = b - r1, b - r2, b - r3
    p1 = lax.dot_general((q * jnp.exp2(jnp.minimum(d1, 0.0))).astype(BF16),
                         (k * jnp.exp2(jnp.minimum(-d1, 0.0))).astype(BF16),
                         NT_DIMS, preferred_element_type=F32)
    p2 = lax.dot_general((q * jnp.exp2(jnp.minimum(d2, 0.0))).astype(BF16),
                         (k * jnp.exp2(jnp.minimum(-d2, 0.0))).astype(BF16),
                         NT_DIMS, preferred_element_type=F32)
    p3 = lax.dot_general((q * jnp.exp2(d3)).astype(BF16), (k * jnp.exp2(-d3)).astype(BF16),
                         NT_DIMS, preferred_element_type=F32)
    att = jnp.where(m1, p1, jnp.where(m2, p2, jnp.where(m3, p3, 0.0)))
    st = st_ref[h]
    o = _bdot(att.astype(BF16), v.astype(BF16))
    o = o + lax.dot_general((q * jnp.exp2(b)).astype(BF16), st.astype(BF16),
                            NT_DIMS, preferred_element_type=F32)
    ks = (k * jnp.exp2(btot - b)).astype(BF16)
    st_ref[h] = st * jnp.exp2(btot) + _bdot(v.T.astype(BF16), ks)
    return o


def _hgrn_scan_kernel(qf_ref, vf_ref, af_ref, qb_ref, vb_ref, ab_ref, lb_ref, of_ref, ob_ref,
                      stf_ref, stb_ref, *, n_chunks, pad):
    C = HG_CHUNK
    c = pl.program_id(1)

    @pl.when(c == 0)
    def _():
        stf_ref[...] = jnp.zeros_like(stf_ref)
        stb_ref[...] = jnp.zeros_like(stb_ref)

    t1 = lax.broadcasted_iota(jnp.int32, (C, 1), 0)
    consts_f = _hgrn_direction_consts(False)
    consts_b = _hgrn_direction_consts(True)
    valid_f = (c * C + t1) >= pad
    valid_b = ((n_chunks - 1 - c) * C + t1) >= pad
    for h in range(HG_HEADS):
        sl = slice(h * HG_FDIM, (h + 1) * HG_FDIM)
        lb = lb_ref[:, sl]
        of_ref[0, :, sl] = _hgrn_head_chunk(_silu_tanh(qf_ref[0, :, sl]), vf_ref[0, :, sl], af_ref[0, :, sl],
                                            lb, valid_f, stf_ref, h, consts_f)
        ob_ref[0, :, sl] = _hgrn_head_chunk(_silu_tanh(qb_ref[0, :, sl]), vb_ref[0, :, sl], ab_ref[0, :, sl],
                                            lb, valid_b, stb_ref, h, consts_b)


def _hgrn_scan(proj, lb, pad):
    B, Tp, _ = proj.shape
    C = HG_CHUNK
    n = Tp // C
    width = HG_HEADS * HG_FDIM

    def fwd(colblk):
        return pl.BlockSpec((1, C, width), lambda b, c: (b, c, colblk))

    def bwd(colblk):
        return pl.BlockSpec((1, C, width), lambda b, c: (b, n - 1 - c, colblk))

    state = pltpu.VMEM((HG_HEADS, HG_IDIM, HG_FDIM), F32)
    return pl.pallas_call(
        functools.partial(_hgrn_scan_kernel, n_chunks=n, pad=pad),
        grid=(B, n),
        in_specs=[fwd(0), fwd(1), fwd(2), bwd(0), bwd(1), bwd(3), _whole((1, width))],
        out_specs=[fwd(0), bwd(0)],
        out_shape=[jax.ShapeDtypeStruct((B, Tp, width), F32), jax.ShapeDtypeStruct((B, Tp, width), F32)],
        scratch_shapes=[state, state],
        compiler_params=_cparams("parallel", "arbitrary"),
        name="hgrn_scan",
    )(proj, proj, proj, proj, proj, proj, lb)


def _hgrn_out_kernel(of_ref, ob_ref, gate_ref, ng_ref, w_ref, h_ref, g_ref, b_ref, o_ref):
    o = of_ref[0] + ob_ref[0]
    parts = []
    for hd in range(HG_HEADS):
        sl = slice(hd * HG_IDIM, (hd + 1) * HG_IDIM)
        parts.append(_rms_norm(o[:, sl], ng_ref[...]))
    on = jnp.concatenate(parts, axis=1) * _silu(gate_ref[0])
    y = _bdot(on.astype(BF16), w_ref[...])
    o_ref[0] = _layer_norm(DN_ALPHA * h_ref[0] + y, g_ref[...], b_ref[...])


def _hgrn_out(o_f, o_b, proj, norm_g, w_out, h, g, b):
    B, Tp, _ = h.shape
    tm = _row_tile(Tp, 640)
    return pl.pallas_call(
        _hgrn_out_kernel,
        grid=(B, Tp // tm),
        in_specs=[_rows(tm, D_MODEL), _rows(tm, D_MODEL), _rows(tm, D_MODEL, col=4),
                  _whole((1, HG_IDIM)), _whole((D_MODEL, D_MODEL)), _rows(tm, D_MODEL),
                  _whole((1, D_MODEL)), _whole((1, D_MODEL))],
        out_specs=_rows(tm, D_MODEL),
        out_shape=jax.ShapeDtypeStruct((B, Tp, D_MODEL), F32),
        compiler_params=_cparams("parallel", "parallel"),
        name="hgrn_out",
    )(o_f, o_b, proj, norm_g, w_out, h, g, b)


def _hgrn_mixer(h, pad, w_in, w_out, norm_g, lb, g, b):
    proj = _matmul(h, w_in, 640)
    o_f, o_b = _hgrn_scan(proj, lb, pad)
    return _hgrn_out(o_f, o_b, proj, norm_g, w_out, h, g, b)


def _pool_kernel(xp_ref, x_ref, xn_ref, w_ref, sc_ref, g_ref, b_ref, o_ref, xe_ref, *, tt, pad, t_len):
    i = pl.program_id(1)
    H = POOL_HALO
    x = x_ref[0]

    def keep(vals, first_row):
        r = first_row + lax.broadcasted_iota(jnp.int32, (vals.shape[0], 1), 0)
        return jnp.where((r >= pad) & (r < pad + t_len), vals, 0.0)

    xe_ref[0:H, :] = keep(xp_ref[0], i * tt - H)
    xe_ref[H:H + tt, :] = keep(x, i * tt)
    xe_ref[H + tt:H + tt + H, :] = keep(xn_ref[0], i * tt + tt)

    tok = i * tt - pad + lax.broadcasted_iota(jnp.int32, (tt, 1), 0)
    ys = []
    for gi, w in enumerate(POOL_WINDOWS):
        sl = slice(gi * POOL_GROUP, (gi + 1) * POOL_GROUP)
        acc = None
        for j in range(-(w // 2), w // 2):
            piece = xe_ref[H + j:H + j + tt, sl]
            acc = piece if acc is None else acc + piece
        lo = jnp.clip(tok - w // 2, 0, t_len)
        hi = jnp.clip(tok + w // 2, 0, t_len)
        cnt = jnp.maximum(hi - lo, 1).astype(F32)
        p = acc / cnt - x[:, sl]
        ys.append(_bdot(p.astype(BF16), w_ref[gi]))
    y = jnp.concatenate(ys, axis=1) * sc_ref[...]
    o_ref[0] = _layer_norm(DN_ALPHA * x + y, g_ref[...], b_ref[...])


def _pool_mixer(h, pad, w_grp, scale, g, b):
    B, Tp, _ = h.shape
    tt = _row_tile(Tp, 640)
    H = POOL_HALO
    per = tt // H
    last = Tp // H - 1
    return pl.pallas_call(
        functools.partial(_pool_kernel, tt=tt, pad=pad, t_len=Tp - pad),
        grid=(B, Tp // tt),
        in_specs=[pl.BlockSpec((1, H, D_MODEL), lambda b, i: (b, jnp.maximum(i * per - 1, 0), 0)),
                  _rows(tt, D_MODEL),
                  pl.BlockSpec((1, H, D_MODEL), lambda b, i: (b, jnp.minimum((i + 1) * per, last), 0)),
                  _whole((len(POOL_WINDOWS), POOL_GROUP, POOL_GROUP)), _whole((1, D_MODEL)),
                  _whole((1, D_MODEL)), _whole((1, D_MODEL))],
        out_specs=_rows(tt, D_MODEL),
        out_shape=jax.ShapeDtypeStruct((B, Tp, D_MODEL), F32),
        scratch_shapes=[pltpu.VMEM((tt + 2 * H, D_MODEL), F32)],
        compiler_params=_cparams("parallel", "parallel"),
        name="pool_mixer",
    )(h, h, h, w_grp, scale, g, b)


def _rope_tables(tp, pad, d):
    inv = 1.0 / (ROPE_THETA ** (jnp.arange(0, d, 2, dtype=F32) / d))
    pos = (jnp.arange(tp) - pad).astype(F32)
    ang = pos[:, None] * inv[None, :]
    cos, sin = jnp.cos(ang), jnp.sin(ang)
    reps = LANE // d
    return (jnp.tile(jnp.concatenate([cos, cos], axis=1), (1, reps)),
            jnp.tile(jnp.concatenate([-sin, sin], axis=1), (1, reps)))


def _swa_proj_kernel(x_ref, w_ref, cos_ref, sin_ref, q_ref, k_ref, v_ref):
    qkv = _bdot(x_ref[0].astype(BF16), w_ref[...])
    cos, sin = cos_ref[...], sin_ref[...]
    nq = SW_HEADS * SW_HEAD_DIM
    nk = SW_KV_HEADS * SW_HEAD_DIM

    def rope(xh):
        return xh * cos + pltpu.roll(xh, SW_HEAD_DIM // 2, 1) * sin

    for hd in range(SW_HEADS):
        sl = slice(hd * SW_HEAD_DIM, (hd + 1) * SW_HEAD_DIM)
        q_ref[0, :, sl] = rope(qkv[:, sl]).astype(BF16)
    for hd in range(SW_KV_HEADS):
        sl = slice(hd * SW_HEAD_DIM, (hd + 1) * SW_HEAD_DIM)
        k_ref[0, :, sl] = rope(qkv[:, nq + hd * SW_HEAD_DIM:nq + (hd + 1) * SW_HEAD_DIM]).astype(BF16)
    v_ref[0] = qkv[:, nq + nk:].astype(BF16)


def _swa_attn_kernel(sink_ref, q_ref, kp_ref, kc_ref, kn_ref, vp_ref, vc_ref, vn_ref,
                     w_ref, h_ref, g_ref, b_ref, o_ref, kcat_ref, vcat_ref, att_ref, *, pad, tp, tt):
    i = pl.program_id(1)
    blk = SW_WINDOW
    kcat_ref[0:blk, :] = kp_ref[0]
    kcat_ref[blk:blk + tt, :] = kc_ref[0]
    kcat_ref[blk + tt:, :] = kn_ref[0]
    vcat_ref[0:blk, :] = vp_ref[0]
    vcat_ref[blk:blk + tt, :] = vc_ref[0]
    vcat_ref[blk + tt:, :] = vn_ref[0]
    scale = SW_HEAD_DIM ** -0.5
    dq = lax.broadcasted_iota(jnp.int32, (blk, 3 * blk), 0)
    dk = lax.broadcasted_iota(jnp.int32, (blk, 3 * blk), 1) - blk

    def one_block(j, carry):
        r0 = pl.multiple_of(j * blk, blk)
        rk = i * tt + r0 + dk
        ok = (rk >= pad) & (rk < tp) & (jnp.abs(dq - dk) <= SW_WINDOW)
        ok = jnp.concatenate([ok] * SW_GROUPS, axis=0)
        for kv in range(SW_KV_HEADS):
            heads = range(kv * SW_GROUPS, (kv + 1) * SW_GROUPS)
            ksl = slice(kv * SW_HEAD_DIM, (kv + 1) * SW_HEAD_DIM)
            qs = jnp.concatenate([q_ref[0, pl.ds(r0, blk), hd * SW_HEAD_DIM:(hd + 1) * SW_HEAD_DIM]
                                  for hd in heads], axis=0)
            sk = jnp.concatenate([jnp.full((blk, 1), sink_ref[hd], F32) for hd in heads], axis=0)
            s = lax.dot_general(qs, kcat_ref[pl.ds(r0, 3 * blk), ksl],
                                NT_DIMS, preferred_element_type=F32) * scale
            s = jnp.where(ok, s, -jnp.inf)
            m = jnp.maximum(jnp.max(s, axis=-1, keepdims=True), sk)
            p = jnp.exp(s - m)
            den = jnp.sum(p, axis=-1, keepdims=True) + jnp.exp(sk - m)
            att = _bdot((p * (1.0 / den)).astype(BF16), vcat_ref[pl.ds(r0, 3 * blk), ksl]).astype(BF16)
            for n, hd in enumerate(heads):
                att_ref[pl.ds(r0, blk), hd * SW_HEAD_DIM:(hd + 1) * SW_HEAD_DIM] = att[n * blk:(n + 1) * blk]
        return carry

    lax.fori_loop(0, tt // blk, one_block, 0, unroll=True)
    y = _bdot(att_ref[...], w_ref[...])
    o_ref[0] = _layer_norm(DN_ALPHA * h_ref[0] + y, g_ref[...], b_ref[...])


def _swa_mixer(h, pad, w_qkv, w_out, sink, g, b):
    B, Tp, _ = h.shape
    cos, sin = _rope_tables(Tp, pad, SW_HEAD_DIM)
    tm = _row_tile(Tp, 640)
    nq = SW_HEADS * SW_HEAD_DIM
    nk = SW_KV_HEADS * SW_HEAD_DIM
    tab = pl.BlockSpec((tm, LANE), lambda b, i: (i, 0))
    q, k, v = pl.pallas_call(
        _swa_proj_kernel,
        grid=(B, Tp // tm),
        in_specs=[_rows(tm, D_MODEL), _whole((D_MODEL, nq + 2 * nk)), tab, tab],
        out_specs=[_rows(tm, nq), _rows(tm, nk), _rows(tm, nk)],
        out_shape=[jax.ShapeDtypeStruct((B, Tp, nq), BF16), jax.ShapeDtypeStruct((B, Tp, nk), BF16),
                   jax.ShapeDtypeStruct((B, Tp, nk), BF16)],
        compiler_params=_cparams("parallel", "parallel"),
        name="swa_proj",
    )(h, w_qkv, cos, sin)

    blk = SW_WINDOW
    nb = Tp // blk
    tt = tm
    per = tt // blk
    prev = pl.BlockSpec((1, blk, nk), lambda b, i: (b, jnp.maximum(i * per - 1, 0), 0))
    cur = pl.BlockSpec((1, tt, nk), lambda b, i: (b, i, 0))
    nxt = pl.BlockSpec((1, blk, nk), lambda b, i: (b, jnp.minimum((i + 1) * per, nb - 1), 0))
    return pl.pallas_call(
        functools.partial(_swa_attn_kernel, pad=pad, tp=Tp, tt=tt),
        grid=(B, Tp // tt),
        in_specs=[pl.BlockSpec(memory_space=pltpu.SMEM),
                  _rows(tt, nq), prev, cur, nxt, prev, cur, nxt,
                  _whole((nq, D_MODEL)), _rows(tt, D_MODEL),
                  _whole((1, D_MODEL)), _whole((1, D_MODEL))],
        out_specs=_rows(tt, D_MODEL),
        out_shape=jax.ShapeDtypeStruct((B, Tp, D_MODEL), F32),
        scratch_shapes=[pltpu.VMEM((tt + 2 * blk, nk), BF16), pltpu.VMEM((tt + 2 * blk, nk), BF16),
                        pltpu.VMEM((tt, nq), BF16)],
        compiler_params=_cparams("parallel", "parallel"),
        name="swa_attn",
    )(sink, q, k, k, k, v, v, v, w_out, h, g, b)


def _mla_proj_kernel(x_ref, wdq_ref, gq_ref, wuq_ref, wdkv_ref, gkv_ref, wukv_ref, cos_ref, sin_ref,
                     q_ref, k_ref, v_ref, *, pad, tm):
    i = pl.program_id(1)
    xb = x_ref[0].astype(BF16)
    cq = _rms_norm(_bdot(xb, wdq_ref[...]), gq_ref[...])
    q = _bdot(cq.astype(BF16), wuq_ref[...])
    ckv = _bdot(xb, wdkv_ref[...])
    c = _rms_norm(ckv[:, :MLA_KV_RANK], gkv_ref[...])
    kv = _bdot(c.astype(BF16), wukv_ref[...])
    cos, sin = cos_ref[...], sin_ref[...]
    lane = lax.broadcasted_iota(jnp.int32, (tm, LANE), 1)
    low = (lane % MLA_ROPE) < MLA_ROPE // 2
    spare = lane == MLA_ROPE

    def rope(xr):
        swapped = jnp.where(low, pltpu.roll(xr, LANE - MLA_ROPE // 2, 1), pltpu.roll(xr, MLA_ROPE // 2, 1))
        return xr * cos + swapped * sin

    filler = (i * tm + lax.broadcasted_iota(jnp.int32, (tm, 1), 0)) < pad
    kr = jnp.where(spare, jnp.where(filler, KEY_MASK_BIAS, 0.0), rope(ckv[:, MLA_KV_RANK:])).astype(BF16)
    for hd in range(MLA_HEADS):
        base = hd * MLA_QK
        q_ref[0, :, base:base + MLA_NOPE] = q[:, base:base + MLA_NOPE].astype(BF16)
        qr = jnp.where(spare, 1.0, rope(q[:, base + MLA_NOPE:base + MLA_QK]))
        q_ref[0, :, base + MLA_NOPE:base + MLA_QK] = qr.astype(BF16)
        k_ref[0, :, base:base + MLA_NOPE] = kv[:, hd * MLA_NOPE:(hd + 1) * MLA_NOPE].astype(BF16)
        k_ref[0, :, base + MLA_NOPE:base + MLA_QK] = kr
    v_ref[0] = kv[:, MLA_HEADS * MLA_NOPE:].astype(BF16)


def _mla_attn_kernel(q_ref, k_ref, v_ref, o_ref, s_ref, p_ref, alpha_ref, linv_ref, m_ref, l_ref, acc_ref,
                     *, tq, kc):
    c = (MLA_NOPE + MLA_ROPE) ** -0.5 * LOG2_E
    tp = k_ref.shape[1]
    nc = tp // kc
    n_items = (tp // tq) * nc
    n_sub = kc // LANE

    def tile_chunk(t):
        if isinstance(t, int):
            return divmod(t, nc)
        qi = lax.div(t, jnp.int32(nc))
        return qi, t - qi * nc

    def rows(i, size):
        return pl.ds(i * size, size) if isinstance(i, int) else pl.ds(pl.multiple_of(i * size, size), size)

    def stage_a(u, st, pos):
        qi, j = tile_chunk(u)
        s_ref[st, pos] = lax.dot_general(q_ref[0, rows(qi, tq), :], k_ref[0, rows(j, kc), :], NT_DIMS,
                                         preferred_element_type=F32)

    def stage_b(u, st, pos):
        _, j = tile_chunk(u)
        m = jnp.where(j == 0, -jnp.inf, m_ref[...])
        parts = [s_ref[st, pos, :, i * LANE:(i + 1) * LANE] for i in range(n_sub)]
        m_new = jnp.maximum(m, jnp.max(functools.reduce(jnp.maximum, parts), axis=-1, keepdims=True))
        alpha = jnp.exp2((m - m_new) * c)
        ps = [jnp.exp2((sp - m_new) * c) for sp in parts]
        l = alpha * l_ref[...] + functools.reduce(jnp.add, ps)
        l_ref[...] = l
        m_ref[...] = m_new
        alpha_ref[st, pos] = alpha
        linv_ref[st, pos] = jnp.broadcast_to(1.0 / jnp.sum(l, axis=-1, keepdims=True), l.shape)
        for i in range(n_sub):
            p_ref[st, pos, :, i * LANE:(i + 1) * LANE] = ps[i].astype(BF16)

    def stage_c(u, st, pos):
        qi, j = tile_chunk(u)
        acc = alpha_ref[st, pos] * acc_ref[...] + _bdot(p_ref[st, pos], v_ref[0, rows(j, kc), :])
        acc_ref[...] = acc
        o_ref[0, rows(qi, tq), :] = (acc * linv_ref[st, pos]).astype(BF16)

    def round_(i, par, static):
        for pos in (0, 1):
            ua, ub, uc = 2 * i + pos, 2 * i - 2 + pos, 2 * i - 4 + pos
            if not static or 0 <= ua < n_items:
                stage_a(ua, par, pos)
            if not static or 0 <= ub < n_items:
                stage_b(ub, 1 - par, pos)
            if not static or 0 <= uc < n_items:
                stage_c(uc, par, pos)

    m_ref[...] = jnp.full(m_ref.shape, -jnp.inf, F32)
    l_ref[...] = jnp.zeros(l_ref.shape, F32)
    acc_ref[...] = jnp.zeros(acc_ref.shape, F32)

    first_full, last_full = 2, (n_items - 2) // 2
    last_round = (n_items + 3) // 2
    steady = range(first_full, last_full + 1) if last_full >= first_full else range(0)
    for i in range(last_round + 1):
        if i not in steady:
            if i == last_full + 1 and len(steady) > 0:
                def body(r, carry):
                    @pl.when(r % 2 == 0)
                    def _():
                        round_(r, 0, False)

                    @pl.when(r % 2 == 1)
                    def _():
                        round_(r, 1, False)
                    return carry
                lax.fori_loop(first_full, last_full + 1, body, 0)
            round_(i, i % 2, True)


def _mla_mixer(h, pad, w_dq, q_norm_g, w_uq, w_dkv, kv_norm_g, w_ukv, w_out, g, b):
    B, Tp, _ = h.shape
    cos, sin = _rope_tables(Tp, pad, MLA_ROPE)
    tm = _row_tile(Tp, 384)
    nqk = MLA_HEADS * MLA_QK
    nv = MLA_HEADS * MLA_V
    tab = pl.BlockSpec((tm, LANE), lambda b, i: (i, 0))
    q, k, v = pl.pallas_call(
        functools.partial(_mla_proj_kernel, pad=pad, tm=tm),
        grid=(B, Tp // tm),
        in_specs=[_rows(tm, D_MODEL), _whole(w_dq.shape), _whole((1, MLA_Q_RANK)), _whole(w_uq.shape),
                  _whole(w_dkv.shape), _whole((1, MLA_KV_RANK)), _whole(w_ukv.shape), tab, tab],
        out_specs=[_rows(tm, nqk), _rows(tm, nqk), _rows(tm, nv)],
        out_shape=[jax.ShapeDtypeStruct((B, Tp, nqk), BF16), jax.ShapeDtypeStruct((B, Tp, nqk), BF16),
                   jax.ShapeDtypeStruct((B, Tp, nv), BF16)],
        compiler_params=_cparams("parallel", "parallel"),
        name="mla_proj",
    )(h, w_dq, q_norm_g, w_uq, w_dkv, kv_norm_g, w_ukv, cos, sin)

    tq = _row_tile(Tp, MLA_Q_TILE_MAX)
    kc = _row_tile(Tp, MLA_KEY_CHUNK_MAX)
    o = pl.pallas_call(
        functools.partial(_mla_attn_kernel, tq=tq, kc=kc),
        grid=(B, MLA_HEADS),
        in_specs=[pl.BlockSpec((1, Tp, MLA_QK), lambda b, hd: (b, 0, hd)),
                  pl.BlockSpec((1, Tp, MLA_QK), lambda b, hd: (b, 0, hd)),
                  pl.BlockSpec((1, Tp, MLA_V), lambda b, hd: (b, 0, hd))],
        out_specs=pl.BlockSpec((1, Tp, MLA_V), lambda b, hd: (b, 0, hd)),
        out_shape=jax.ShapeDtypeStruct((B, Tp, nv), BF16),
        scratch_shapes=[pltpu.VMEM((2, 2, tq, kc), F32), pltpu.VMEM((2, 2, tq, kc), BF16),
                        pltpu.VMEM((2, 2, tq, LANE), F32), pltpu.VMEM((2, 2, tq, LANE), F32),
                        pltpu.VMEM((tq, LANE), F32), pltpu.VMEM((tq, LANE), F32),
                        pltpu.VMEM((tq, MLA_V), F32)],
        compiler_params=_cparams("parallel", "parallel"),
        name="mla_attn",
    )(q, k, v)
    return _proj_ln(o, w_out, h, g, b)


def _prepare_params(p):
    bf = lambda a: a.astype(BF16)
    row = lambda a: a.astype(F32).reshape(1, -1)
    out = dict(p)
    for name in ('a_w_in', 'a_w_out', 'b_w_grp', 'c_w_qkv', 'c_w_out', 'd_w_dq', 'd_w_out',
                 'ffn_w_gu', 'ffn_w_down'):
        out[name] = bf(p[name])
    n_d = p['d_w_uq'].shape[0]
    wuq = p['d_w_uq'].reshape(n_d, MLA_Q_RANK, MLA_HEADS, MLA_NOPE + MLA_ROPE)
    wuq = jnp.pad(wuq, ((0, 0), (0, 0), (0, 0), (0, MLA_QK - MLA_NOPE - MLA_ROPE)))
    out['d_w_uq'] = bf(wuq.reshape(n_d, MLA_Q_RANK, MLA_HEADS * MLA_QK))
    out['d_w_dkv'] = bf(jnp.pad(p['d_w_dkv'], ((0, 0), (0, 0), (0, LANE - MLA_ROPE))))
    wukv = p['d_w_ukv'].reshape(n_d, MLA_KV_RANK, MLA_HEADS, MLA_NOPE + MLA_V)
    out['d_w_ukv'] = bf(jnp.concatenate(
        [wukv[..., :MLA_NOPE].reshape(n_d, MLA_KV_RANK, MLA_HEADS * MLA_NOPE),
         wukv[..., MLA_NOPE:].reshape(n_d, MLA_KV_RANK, MLA_HEADS * MLA_V)], axis=-1))
    out['lb_all'] = jnp.cumsum(jax.nn.softmax(p['hg_lb_logits'].astype(F32), axis=0), axis=0)
    return out


def _trunk(x, p):
    B, S, _ = x.shape
    T = N_META + S
    Tp = -(-T // LANE) * LANE
    pad = Tp - T
    meta = jnp.broadcast_to(p['meta_tokens'].astype(x.dtype)[None], (B, N_META, D_MODEL))
    h = jnp.concatenate([jnp.zeros((B, pad, D_MODEL), x.dtype), meta, x], axis=1)
    row = lambda a: a.astype(F32).reshape(1, -1)
    for i in range(DEPTH):
        kind, j = i % N_MIXERS, i // N_MIXERS
        g0, b0 = row(p['ln_g'][i, 0]), row(p['ln_b'][i, 0])
        if kind == 0:
            h = _hgrn_mixer(h, pad, p['a_w_in'][j], p['a_w_out'][j], row(p['a_norm_g'][j]),
                            row(p['lb_all'][i]), g0, b0)
        elif kind == 1:
            h = _pool_mixer(h, pad, p['b_w_grp'][j], row(p['b_scale'][j]), g0, b0)
        elif kind == 2:
            h = _swa_mixer(h, pad, p['c_w_qkv'][j], p['c_w_out'][j], p['c_sink'][j].astype(F32), g0, b0)
        else:
            h = _mla_mixer(h, pad, p['d_w_dq'][j], row(p['d_q_norm_g'][j]), p['d_w_uq'][j],
                           p['d_w_dkv'][j], row(p['d_kv_norm_g'][j]), p['d_w_ukv'][j],
                           p['d_w_out'][j], g0, b0)
        h = _ffn(h, p['ffn_w_gu'][i], p['ffn_w_down'][i], row(p['ln_g'][i, 1]), row(p['ln_b'][i, 1]),
                 skip_rows=pad + N_META if i == DEPTH - 1 else 0)
    return h


def kernel(x_prompt, x_sample, meta_tokens, hg_lb_logits, a_w_in, a_w_out, a_norm_g, b_w_grp, b_scale,
           c_w_qkv, c_w_out, c_sink, d_w_dq, d_q_norm_g, d_w_uq, d_w_dkv, d_kv_norm_g, d_w_ukv, d_w_out,
           ffn_w_gu, ffn_w_down, ln_g, ln_b):
    params = _prepare_params({
        'meta_tokens': meta_tokens, 'hg_lb_logits': hg_lb_logits,
        'a_w_in': a_w_in, 'a_w_out': a_w_out, 'a_norm_g': a_norm_g,
        'b_w_grp': b_w_grp, 'b_scale': b_scale,
        'c_w_qkv': c_w_qkv, 'c_w_out': c_w_out, 'c_sink': c_sink,
        'd_w_dq': d_w_dq, 'd_q_norm_g': d_q_norm_g, 'd_w_uq': d_w_uq, 'd_w_dkv': d_w_dkv,
        'd_kv_norm_g': d_kv_norm_g, 'd_w_ukv': d_w_ukv, 'd_w_out': d_w_out,
        'ffn_w_gu': ffn_w_gu, 'ffn_w_down': ffn_w_down, 'ln_g': ln_g, 'ln_b': ln_b,
    })
    return (_trunk(x_prompt, params), _trunk(x_sample, params))
```

```python
import functools

import jax
import jax.numpy as jnp
from jax import lax
from jax.experimental import pallas as pl
from jax.experimental.pallas import tpu as pltpu

F32 = jnp.float32
BF16 = jnp.bfloat16

D_MODEL = 1024
DEPTH = 4
N_META = 16
N_MIXERS = 4
DN_ALPHA = (2 * DEPTH) ** 0.25
LN_EPS = 1e-5
RMS_EPS = 1e-6
ROPE_THETA = 10000.0

HG_HEADS = 8
HG_FDIM = 128
HG_IDIM = D_MODEL // HG_HEADS
POOL_WINDOWS = (2, 4, 8, 16)
POOL_GROUP = D_MODEL // len(POOL_WINDOWS)
POOL_HALO = 8
SW_HEADS = 8
SW_KV_HEADS = 2
SW_GROUPS = SW_HEADS // SW_KV_HEADS
SW_HEAD_DIM = D_MODEL // SW_HEADS
SW_WINDOW = 128
MLA_HEADS = 16
MLA_Q_RANK = 256
MLA_KV_RANK = 256
MLA_NOPE = 128
MLA_ROPE = 64
MLA_V = 128
MLA_QK = 256
MLA_Q_TILE_MAX = 640
MLA_KEY_CHUNK_MAX = 1536
D_FF = ((8 * D_MODEL + 3 * 256 - 1) // (3 * 256)) * 256

LANE = 128
HG_CHUNK = 128
VMEM_LIMIT_BYTES = 56 * 1024 * 1024
KEY_MASK_BIAS = -1e30
LOG2_E = 1.4426950408889634

NT_DIMS = (((1,), (1,)), ((), ()))


def _cparams(*sem):
    return pltpu.CompilerParams(dimension_semantics=sem, vmem_limit_bytes=VMEM_LIMIT_BYTES)


def _row_tile(tp, max_rows):
    best = LANE
    for t in range(LANE, max_rows + 1, LANE):
        if tp % t == 0:
            best = t
    return best


def _whole(shape):
    nd = len(shape)
    return pl.BlockSpec(shape, lambda *_: (0,) * nd, pipeline_mode=pl.Buffered(1))


def _rows(tm, width, col=0):
    return pl.BlockSpec((1, tm, width), lambda b, i: (b, i, col))


def _layer_norm(z, g, b):
    mu = jnp.mean(z, axis=-1, keepdims=True)
    zc = z - mu
    var = jnp.mean(zc * zc, axis=-1, keepdims=True)
    return zc * lax.rsqrt(var + LN_EPS) * g + b


def _rms_norm(x, g):
    return x * lax.rsqrt(jnp.mean(x * x, axis=-1, keepdims=True) + RMS_EPS) * g


def _sigmoid(x):
    return 1.0 / (1.0 + jnp.exp(-x))


def _silu_tanh(x):
    hx = 0.5 * x
    return hx * jnp.tanh(hx) + hx


def _silu(x):
    return x * _sigmoid(x)


def _bdot(a, b):
    return jnp.dot(a, b, preferred_element_type=F32)


def _matmul_kernel(x_ref, w_ref, o_ref):
    o_ref[0] = _bdot(x_ref[0].astype(BF16), w_ref[...])


def _matmul(x, w, max_rows):
    B, Tp, K = x.shape
    N = w.shape[1]
    tm = _row_tile(Tp, max_rows)
    return pl.pallas_call(
        _matmul_kernel,
        grid=(B, Tp // tm),
        in_specs=[_rows(tm, K), _whole((K, N))],
        out_specs=_rows(tm, N),
        out_shape=jax.ShapeDtypeStruct((B, Tp, N), F32),
        compiler_params=_cparams("parallel", "parallel"),
        name="in_proj",
    )(x, w)


def _proj_ln_kernel(a_ref, w_ref, h_ref, g_ref, b_ref, o_ref):
    y = _bdot(a_ref[0], w_ref[...])
    o_ref[0] = _layer_norm(DN_ALPHA * h_ref[0] + y, g_ref[...], b_ref[...])


def _proj_ln(a, w, h, g, b, max_rows=640):
    B, Tp, K = a.shape
    tm = _row_tile(Tp, max_rows)
    return pl.pallas_call(
        _proj_ln_kernel,
        grid=(B, Tp // tm),
        in_specs=[_rows(tm, K), _whole((K, D_MODEL)), _rows(tm, D_MODEL),
                  _whole((1, D_MODEL)), _whole((1, D_MODEL))],
        out_specs=_rows(tm, D_MODEL),
        out_shape=jax.ShapeDtypeStruct((B, Tp, D_MODEL), F32),
        compiler_params=_cparams("parallel", "parallel"),
        name="out_proj_ln",
    )(a, w, h, g, b)


FFN_CHUNKS = 11


def _ffn_kernel(x_ref, wgu_ref, wd_ref, g_ref, b_ref, o_ref):
    x = x_ref[...].reshape(x_ref.shape[-2:])
    xb = x.astype(BF16)
    ch = D_FF // FFN_CHUNKS
    y = None
    for c in range(FFN_CHUNKS):
        gate = _bdot(xb, wgu_ref[:, c * ch:(c + 1) * ch])
        up = _bdot(xb, wgu_ref[:, D_FF + c * ch:D_FF + (c + 1) * ch])
        act = (_silu(gate) * up).astype(BF16)
        part = _bdot(act, wd_ref[c * ch:(c + 1) * ch, :])
        y = part if y is None else y + part
    o_ref[0] = _layer_norm(DN_ALPHA * x + y, g_ref[...], b_ref[...])


def _ffn(h, wgu, wd, g, b, skip_rows=0):
    B, Tp, _ = h.shape
    rows_out = Tp - skip_rows
    if skip_rows:
        tm = _row_tile(rows_out, 640)
        h = h.reshape(B * Tp, D_MODEL)
        x_spec = pl.BlockSpec((pl.Element(tm), pl.Element(D_MODEL)),
                              lambda b, i: (pl.multiple_of(b * Tp + skip_rows + i * tm, LANE), 0))
    else:
        tm = _row_tile(Tp, 1408)
        x_spec = _rows(tm, D_MODEL)
    return pl.pallas_call(
        _ffn_kernel,
        grid=(B, rows_out // tm),
        in_specs=[x_spec, _whole((D_MODEL, 2 * D_FF)), _whole((D_FF, D_MODEL)),
                  _whole((1, D_MODEL)), _whole((1, D_MODEL))],
        out_specs=_rows(tm, D_MODEL),
        out_shape=jax.ShapeDtypeStruct((B, rows_out, D_MODEL), F32),
        compiler_params=_cparams("parallel", "parallel"),
        name="ffn",
    )(h, wgu, wd, g, b)


def _hgrn_direction_consts(reverse):
    C = HG_CHUNK
    row = lax.broadcasted_iota(jnp.int32, (C, C), 0)
    col = lax.broadcasted_iota(jnp.int32, (C, C), 1)
    rb, cb = row // 32, col // 32
    if reverse:
        cum = col >= row
        m1 = (row < 64) & (col >= 64)
        m2 = (rb // 2 == cb // 2) & (rb % 2 == 0) & (cb % 2 == 1)
        m3 = (rb == cb) & (col >= row)
        refs = (64, (32, 96), (16, 48, 80, 112), 0)
    else:
        cum = col <= row
        m1 = (row >= 64) & (col < 64)
        m2 = (rb // 2 == cb // 2) & (rb % 2 == 1) & (cb % 2 == 0)
        m3 = (rb == cb) & (col <= row)
        refs = (63, (31, 95), (15, 47, 79, 111), C - 1)
    cum = cum.astype(BF16)
    return jnp.concatenate([cum, cum], axis=1), (m1, m2, m3), refs


def _hgrn_head_chunk(q, v, a, lb, valid, st_ref, h, consts):
    cum2, (m1, m2, m3), (ref1, ref2, ref3, last) = consts
    t1 = lax.broadcasted_iota(jnp.int32, (HG_CHUNK, 1), 0)
    c0, c1 = 0.5 * (1.0 + lb), 0.5 * (1.0 - lb)
    ct = c1 * jnp.tanh(0.5 * a)
    f = c0 + ct
    k = jnp.where(valid, c1 - ct, 0.0)
    lf = jnp.where(valid, jnp.log(f) * LOG2_E, 0.0)
    lf0 = lf.astype(BF16)
    lf1 = (lf - lf0.astype(F32)).astype(BF16)
    b = _bdot(cum2, jnp.concatenate([lf0, lf1], axis=0))

    def brow(i):
        return b[i:i + 1, :]

    r1 = brow(ref1)
    r2 = jnp.where(t1 < 64, brow(ref2[0]), brow(ref2[1]))
    r3 = jnp.where(t1 < 32, brow(ref3[0]),
                   jnp.where(t1 < 64, brow(ref3[1]),
                             jnp.where(t1 < 96, brow(ref3[2]), brow(ref3[3]))))
    btot = brow(last)
    d1, d2, d3 = b - r1, b - r2, b - r3
    p1 = lax.dot_general((q * jnp.exp2(jnp.minimum(d1, 0.0))).astype(BF16),
                         (k * jnp.exp2(jnp.minimum(-d1, 0.0))).astype(BF16),
                         NT_DIMS, preferred_element_type=F32)
    p2 = lax.dot_general((q * jnp.exp2(jnp.minimum(d2, 0.0))).astype(BF16),
                         (k * jnp.exp2(jnp.minimum(-d2, 0.0))).astype(BF16),
                         NT_DIMS, preferred_element_type=F32)
    p3 = lax.dot_general((q * jnp.exp2(d3)).astype(BF16), (k * jnp.exp2(-d3)).astype(BF16),
                         NT_DIMS, preferred_element_type=F32)
    att = jnp.where(m1, p1, jnp.where(m2, p2, jnp.where(m3, p3, 0.0)))
    st = st_ref[h]
    o = _bdot(att.astype(BF16), v.astype(BF16))
    o = o + lax.dot_general((q * jnp.exp2(b)).astype(BF16), st.astype(BF16),
                            NT_DIMS, preferred_element_type=F32)
    ks = (k * jnp.exp2(btot - b)).astype(BF16)
    st_ref[h] = st * jnp.exp2(btot) + _bdot(v.T.astype(BF16), ks)
    return o


def _hgrn_scan_kernel(qf_ref, vf_ref, af_ref, qb_ref, vb_ref, ab_ref, lb_ref, of_ref, ob_ref,
                      stf_ref, stb_ref, *, n_chunks, pad):
    C = HG_CHUNK
    c = pl.program_id(1)

    @pl.when(c == 0)
    def _():
        stf_ref[...] = jnp.zeros_like(stf_ref)
        stb_ref[...] = jnp.zeros_like(stb_ref)

    t1 = lax.broadcasted_iota(jnp.int32, (C, 1), 0)
    consts_f = _hgrn_direction_consts(False)
    consts_b = _hgrn_direction_consts(True)
    valid_f = (c * C + t1) >= pad
    valid_b = ((n_chunks - 1 - c) * C + t1) >= pad
    for h in range(HG_HEADS):
        sl = slice(h * HG_FDIM, (h + 1) * HG_FDIM)
        lb = lb_ref[:, sl]
        of_ref[0, :, sl] = _hgrn_head_chunk(_silu_tanh(qf_ref[0, :, sl]), vf_ref[0, :, sl], af_ref[0, :, sl],
                                            lb, valid_f, stf_ref, h, consts_f)
        ob_ref[0, :, sl] = _hgrn_head_chunk(_silu_tanh(qb_ref[0, :, sl]), vb_ref[0, :, sl], ab_ref[0, :, sl],
                                            lb, valid_b, stb_ref, h, consts_b)


def _hgrn_scan(proj, lb, pad):
    B, Tp, _ = proj.shape
    C = HG_CHUNK
    n = Tp // C
    width = HG_HEADS * HG_FDIM

    def fwd(colblk):
        return pl.BlockSpec((1, C, width), lambda b, c: (b, c, colblk))

    def bwd(colblk):
        return pl.BlockSpec((1, C, width), lambda b, c: (b, n - 1 - c, colblk))

    state = pltpu.VMEM((HG_HEADS, HG_IDIM, HG_FDIM), F32)
    return pl.pallas_call(
        functools.partial(_hgrn_scan_kernel, n_chunks=n, pad=pad),
        grid=(B, n),
        in_specs=[fwd(0), fwd(1), fwd(2), bwd(0), bwd(1), bwd(3), _whole((1, width))],
        out_specs=[fwd(0), bwd(0)],
        out_shape=[jax.ShapeDtypeStruct((B, Tp, width), F32), jax.ShapeDtypeStruct((B, Tp, width), F32)],
        scratch_shapes=[state, state],
        compiler_params=_cparams("parallel", "arbitrary"),
        name="hgrn_scan",
    )(proj, proj, proj, proj, proj, proj, lb)


def _hgrn_out_kernel(of_ref, ob_ref, gate_ref, ng_ref, w_ref, h_ref, g_ref, b_ref, o_ref):
    o = of_ref[0] + ob_ref[0]
    parts = []
    for hd in range(HG_HEADS):
        sl = slice(hd * HG_IDIM, (hd + 1) * HG_IDIM)
        parts.append(_rms_norm(o[:, sl], ng_ref[...]))
    on = jnp.concatenate(parts, axis=1) * _silu(gate_ref[0])
    y = _bdot(on.astype(BF16), w_ref[...])
    o_ref[0] = _layer_norm(DN_ALPHA * h_ref[0] + y, g_ref[...], b_ref[...])


def _hgrn_out(o_f, o_b, proj, norm_g, w_out, h, g, b):
    B, Tp, _ = h.shape
    tm = _row_tile(Tp, 640)
    return pl.pallas_call(
        _hgrn_out_kernel,
        grid=(B, Tp // tm),
        in_specs=[_rows(tm, D_MODEL), _rows(tm, D_MODEL), _rows(tm, D_MODEL, col=4),
                  _whole((1, HG_IDIM)), _whole((D_MODEL, D_MODEL)), _rows(tm, D_MODEL),
                  _whole((1, D_MODEL)), _whole((1, D_MODEL))],
        out_specs=_rows(tm, D_MODEL),
        out_shape=jax.ShapeDtypeStruct((B, Tp, D_MODEL), F32),
        compiler_params=_cparams("parallel", "parallel"),
        name="hgrn_out",
    )(o_f, o_b, proj, norm_g, w_out, h, g, b)


def _hgrn_mixer(h, pad, w_in, w_out, norm_g, lb, g, b):
    proj = _matmul(h, w_in, 640)
    o_f, o_b = _hgrn_scan(proj, lb, pad)
    return _hgrn_out(o_f, o_b, proj, norm_g, w_out, h, g, b)


def _pool_kernel(xp_ref, x_ref, xn_ref, w_ref, sc_ref, g_ref, b_ref, o_ref, xe_ref, *, tt, pad, t_len):
    i = pl.program_id(1)
    H = POOL_HALO
    x = x_ref[0]

    def keep(vals, first_row):
        r = first_row + lax.broadcasted_iota(jnp.int32, (vals.shape[0], 1), 0)
        return jnp.where((r >= pad) & (r < pad + t_len), vals, 0.0)

    xe_ref[0:H, :] = keep(xp_ref[0], i * tt - H)
    xe_ref[H:H + tt, :] = keep(x, i * tt)
    xe_ref[H + tt:H + tt + H, :] = keep(xn_ref[0], i * tt + tt)

    tok = i * tt - pad + lax.broadcasted_iota(jnp.int32, (tt, 1), 0)
    ys = []
    for gi, w in enumerate(POOL_WINDOWS):
        sl = slice(gi * POOL_GROUP, (gi + 1) * POOL_GROUP)
        acc = None
        for j in range(-(w // 2), w // 2):
            piece = xe_ref[H + j:H + j + tt, sl]
            acc = piece if acc is None else acc + piece
        lo = jnp.clip(tok - w // 2, 0, t_len)
        hi = jnp.clip(tok + w // 2, 0, t_len)
        inv_cnt = 1.0 / jnp.maximum(hi - lo, 1).astype(F32)
        p = acc * inv_cnt - x[:, sl]
        ys.append(_bdot(p.astype(BF16), w_ref[gi]))
    y = jnp.concatenate(ys, axis=1) * sc_ref[...]
    o_ref[0] = _layer_norm(DN_ALPHA * x + y, g_ref[...], b_ref[...])


def _pool_mixer(h, pad, w_grp, scale, g, b):
    B, Tp, _ = h.shape
    tt = _row_tile(Tp, 640)
    H = POOL_HALO
    per = tt // H
    last = Tp // H - 1
    return pl.pallas_call(
        functools.partial(_pool_kernel, tt=tt, pad=pad, t_len=Tp - pad),
        grid=(B, Tp // tt),
        in_specs=[pl.BlockSpec((1, H, D_MODEL), lambda b, i: (b, jnp.maximum(i * per - 1, 0), 0)),
                  _rows(tt, D_MODEL),
                  pl.BlockSpec((1, H, D_MODEL), lambda b, i: (b, jnp.minimum((i + 1) * per, last), 0)),
                  _whole((len(POOL_WINDOWS), POOL_GROUP, POOL_GROUP)), _whole((1, D_MODEL)),
                  _whole((1, D_MODEL)), _whole((1, D_MODEL))],
        out_specs=_rows(tt, D_MODEL),
        out_shape=jax.ShapeDtypeStruct((B, Tp, D_MODEL), F32),
        scratch_shapes=[pltpu.VMEM((tt + 2 * H, D_MODEL), F32)],
        compiler_params=_cparams("parallel", "parallel"),
        name="pool_mixer",
    )(h, h, h, w_grp, scale, g, b)


def _rope_tables(tp, pad, d):
    inv = 1.0 / (ROPE_THETA ** (jnp.arange(0, d, 2, dtype=F32) / d))
    pos = (jnp.arange(tp) - pad).astype(F32)
    ang = pos[:, None] * inv[None, :]
    cos, sin = jnp.cos(ang), jnp.sin(ang)
    reps = LANE // d
    return (jnp.tile(jnp.concatenate([cos, cos], axis=1), (1, reps)),
            jnp.tile(jnp.concatenate([-sin, sin], axis=1), (1, reps)))


def _swa_proj_kernel(x_ref, w_ref, cos_ref, sin_ref, q_ref, k_ref, v_ref):
    qkv = _bdot(x_ref[0].astype(BF16), w_ref[...])
    cos, sin = cos_ref[...], sin_ref[...]
    nq = SW_HEADS * SW_HEAD_DIM
    nk = SW_KV_HEADS * SW_HEAD_DIM

    def rope(xh):
        return xh * cos + pltpu.roll(xh, SW_HEAD_DIM // 2, 1) * sin

    for hd in range(SW_HEADS):
        sl = slice(hd * SW_HEAD_DIM, (hd + 1) * SW_HEAD_DIM)
        q_ref[0, :, sl] = rope(qkv[:, sl]).astype(BF16)
    for hd in range(SW_KV_HEADS):
        sl = slice(hd * SW_HEAD_DIM, (hd + 1) * SW_HEAD_DIM)
        k_ref[0, :, sl] = rope(qkv[:, nq + hd * SW_HEAD_DIM:nq + (hd + 1) * SW_HEAD_DIM]).astype(BF16)
    v_ref[0] = qkv[:, nq + nk:].astype(BF16)


def _swa_attn_kernel(sink_ref, q_ref, kp_ref, kc_ref, kn_ref, vp_ref, vc_ref, vn_ref,
                     w_ref, h_ref, g_ref, b_ref, o_ref, kcat_ref, vcat_ref, att_ref, s_ref, p_ref,
                     *, pad, tp, tt):
    i = pl.program_id(1)
    blk = SW_WINDOW
    kcat_ref[0:blk, :] = kp_ref[0]
    kcat_ref[blk:blk + tt, :] = kc_ref[0]
    kcat_ref[blk + tt:, :] = kn_ref[0]
    vcat_ref[0:blk, :] = vp_ref[0]
    vcat_ref[blk:blk + tt, :] = vc_ref[0]
    vcat_ref[blk + tt:, :] = vn_ref[0]
    scale = SW_HEAD_DIM ** -0.5
    dq = lax.broadcasted_iota(jnp.int32, (blk, 3 * blk), 0)
    dk = lax.broadcasted_iota(jnp.int32, (blk, 3 * blk), 1) - blk

    units = [(j, kv) for j in range(tt // blk) for kv in range(SW_KV_HEADS)]

    def heads_of(kv):
        return range(kv * SW_GROUPS, (kv + 1) * SW_GROUPS)

    for n, (j, kv) in enumerate(units):
        qs = jnp.concatenate([q_ref[0, j * blk:(j + 1) * blk, hd * SW_HEAD_DIM:(hd + 1) * SW_HEAD_DIM]
                              for hd in heads_of(kv)], axis=0)
        s_ref[n] = lax.dot_general(qs, kcat_ref[j * blk:(j + 3) * blk, kv * SW_HEAD_DIM:(kv + 1) * SW_HEAD_DIM],
                                   NT_DIMS, preferred_element_type=F32)
    c = scale * LOG2_E
    band = jnp.abs(dq - dk) <= SW_WINDOW
    ok_of_block = {}
    for n, (j, kv) in enumerate(units):
        if j not in ok_of_block:
            rk = i * tt + j * blk + dk[:1]
            ok = band & ((rk >= pad) & (rk < tp))
            ok_of_block[j] = jnp.concatenate([ok] * SW_GROUPS, axis=0)
        sk = jnp.concatenate([jnp.full((blk, 1), sink_ref[hd] * (1.0 / scale), F32) for hd in heads_of(kv)],
                             axis=0)
        s = jnp.where(ok_of_block[j], s_ref[n], -jnp.inf)
        m = jnp.maximum(jnp.max(s, axis=-1, keepdims=True), sk)
        p = jnp.exp2((s - m) * c)
        den = jnp.sum(p, axis=-1, keepdims=True) + jnp.exp2((sk - m) * c)
        p_ref[n] = (p * (1.0 / den)).astype(BF16)
    for n, (j, kv) in enumerate(units):
        att = _bdot(p_ref[n], vcat_ref[j * blk:(j + 3) * blk, kv * SW_HEAD_DIM:(kv + 1) * SW_HEAD_DIM])
        att = att.astype(BF16)
        for g, hd in enumerate(heads_of(kv)):
            att_ref[j * blk:(j + 1) * blk, hd * SW_HEAD_DIM:(hd + 1) * SW_HEAD_DIM] = att[g * blk:(g + 1) * blk]
    y = _bdot(att_ref[...], w_ref[...])
    o_ref[0] = _layer_norm(DN_ALPHA * h_ref[0] + y, g_ref[...], b_ref[...])


def _swa_mixer(h, pad, w_qkv, w_out, sink, g, b):
    B, Tp, _ = h.shape
    cos, sin = _rope_tables(Tp, pad, SW_HEAD_DIM)
    tm = _row_tile(Tp, 640)
    nq = SW_HEADS * SW_HEAD_DIM
    nk = SW_KV_HEADS * SW_HEAD_DIM
    tab = pl.BlockSpec((tm, LANE), lambda b, i: (i, 0))
    q, k, v = pl.pallas_call(
        _swa_proj_kernel,
        grid=(B, Tp // tm),
        in_specs=[_rows(tm, D_MODEL), _whole((D_MODEL, nq + 2 * nk)), tab, tab],
        out_specs=[_rows(tm, nq), _rows(tm, nk), _rows(tm, nk)],
        out_shape=[jax.ShapeDtypeStruct((B, Tp, nq), BF16), jax.ShapeDtypeStruct((B, Tp, nk), BF16),
                   jax.ShapeDtypeStruct((B, Tp, nk), BF16)],
        compiler_params=_cparams("parallel", "parallel"),
        name="swa_proj",
    )(h, w_qkv, cos, sin)

    blk = SW_WINDOW
    nb = Tp // blk
    tt = tm
    per = tt // blk
    prev = pl.BlockSpec((1, blk, nk), lambda b, i: (b, jnp.maximum(i * per - 1, 0), 0))
    cur = pl.BlockSpec((1, tt, nk), lambda b, i: (b, i, 0))
    nxt = pl.BlockSpec((1, blk, nk), lambda b, i: (b, jnp.minimum((i + 1) * per, nb - 1), 0))
    return pl.pallas_call(
        functools.partial(_swa_attn_kernel, pad=pad, tp=Tp, tt=tt),
        grid=(B, Tp // tt),
        in_specs=[pl.BlockSpec(memory_space=pltpu.SMEM),
                  _rows(tt, nq), prev, cur, nxt, prev, cur, nxt,
                  _whole((nq, D_MODEL)), _rows(tt, D_MODEL),
                  _whole((1, D_MODEL)), _whole((1, D_MODEL))],
        out_specs=_rows(tt, D_MODEL),
        out_shape=jax.ShapeDtypeStruct((B, Tp, D_MODEL), F32),
        scratch_shapes=[pltpu.VMEM((tt + 2 * blk, nk), BF16), pltpu.VMEM((tt + 2 * blk, nk), BF16),
                        pltpu.VMEM((tt, nq), BF16),
                        pltpu.VMEM((per * SW_KV_HEADS, SW_GROUPS * blk, 3 * blk), F32),
                        pltpu.VMEM((per * SW_KV_HEADS, SW_GROUPS * blk, 3 * blk), BF16)],
        compiler_params=_cparams("parallel", "parallel"),
        name="swa_attn",
    )(sink, q, k, k, k, v, v, v, w_out, h, g, b)


def _mla_proj_kernel(x_ref, wdq_ref, gq_ref, wuq_ref, wdkv_ref, gkv_ref, wukv_ref, cos_ref, sin_ref,
                     q_ref, k_ref, v_ref, *, pad, tm):
    i = pl.program_id(1)
    xb = x_ref[0].astype(BF16)
    cq = _rms_norm(_bdot(xb, wdq_ref[...]), gq_ref[...])
    q = _bdot(cq.astype(BF16), wuq_ref[...])
    ckv = _bdot(xb, wdkv_ref[...])
    c = _rms_norm(ckv[:, :MLA_KV_RANK], gkv_ref[...])
    kv = _bdot(c.astype(BF16), wukv_ref[...])
    cos, sin = cos_ref[...], sin_ref[...]
    lane = lax.broadcasted_iota(jnp.int32, (tm, LANE), 1)
    low = (lane % MLA_ROPE) < MLA_ROPE // 2
    spare = lane == MLA_ROPE

    def rope(xr):
        swapped = jnp.where(low, pltpu.roll(xr, LANE - MLA_ROPE // 2, 1), pltpu.roll(xr, MLA_ROPE // 2, 1))
        return xr * cos + swapped * sin

    filler = (i * tm + lax.broadcasted_iota(jnp.int32, (tm, 1), 0)) < pad
    kr = jnp.where(spare, jnp.where(filler, KEY_MASK_BIAS, 0.0), rope(ckv[:, MLA_KV_RANK:])).astype(BF16)
    for hd in range(MLA_HEADS):
        base = hd * MLA_QK
        q_ref[0, :, base:base + MLA_NOPE] = q[:, base:base + MLA_NOPE].astype(BF16)
        qr = jnp.where(spare, 1.0, rope(q[:, base + MLA_NOPE:base + MLA_QK]))
        q_ref[0, :, base + MLA_NOPE:base + MLA_QK] = qr.astype(BF16)
        k_ref[0, :, base:base + MLA_NOPE] = kv[:, hd * MLA_NOPE:(hd + 1) * MLA_NOPE].astype(BF16)
        k_ref[0, :, base + MLA_NOPE:base + MLA_QK] = kr
    v_ref[0] = kv[:, MLA_HEADS * MLA_NOPE:].astype(BF16)


def _mla_attn_kernel(q_ref, k_ref, v_ref, o_ref, s_ref, p_ref, alpha_ref, linv_ref, m_ref, l_ref, acc_ref,
                     *, tq, kc):
    c = (MLA_NOPE + MLA_ROPE) ** -0.5 * LOG2_E
    tp = k_ref.shape[1]
    nc = tp // kc
    n_items = (tp // tq) * nc
    n_sub = kc // LANE

    def tile_chunk(t):
        if isinstance(t, int):
            return divmod(t, nc)
        qi = lax.div(t, jnp.int32(nc))
        return qi, t - qi * nc

    def rows(i, size):
        return pl.ds(i * size, size) if isinstance(i, int) else pl.ds(pl.multiple_of(i * size, size), size)

    def stage_a(u, st, pos):
        qi, j = tile_chunk(u)
        s_ref[st, pos] = lax.dot_general(q_ref[0, rows(qi, tq), :], k_ref[0, rows(j, kc), :], NT_DIMS,
                                         preferred_element_type=F32)

    def stage_b(u, st, pos):
        _, j = tile_chunk(u)
        m = jnp.where(j == 0, -jnp.inf, m_ref[...])
        parts = [s_ref[st, pos, :, i * LANE:(i + 1) * LANE] for i in range(n_sub)]
        m_new = jnp.maximum(m, jnp.max(functools.reduce(jnp.maximum, parts), axis=-1, keepdims=True))
        alpha = jnp.exp2((m - m_new) * c)
        ps = [jnp.exp2((sp - m_new) * c) for sp in parts]
        l = alpha * l_ref[...] + functools.reduce(jnp.add, ps)
        l_ref[...] = l
        m_ref[...] = m_new
        alpha_ref[st, pos] = alpha
        linv_ref[st, pos] = jnp.broadcast_to(1.0 / jnp.sum(l, axis=-1, keepdims=True), l.shape)
        for i in range(n_sub):
            p_ref[st, pos, :, i * LANE:(i + 1) * LANE] = ps[i].astype(BF16)

    def stage_c(u, st, pos):
        qi, j = tile_chunk(u)
        acc = alpha_ref[st, pos] * acc_ref[...] + _bdot(p_ref[st, pos], v_ref[0, rows(j, kc), :])
        acc_ref[...] = acc
        o_ref[0, rows(qi, tq), :] = (acc * linv_ref[st, pos]).astype(BF16)

    def round_(i, par, static):
        for pos in (0, 1):
            ua, ub, uc = 2 * i + pos, 2 * i - 2 + pos, 2 * i - 4 + pos
            if not static or 0 <= ua < n_items:
                stage_a(ua, par, pos)
            if not static or 0 <= ub < n_items:
                stage_b(ub, 1 - par, pos)
            if not static or 0 <= uc < n_items:
                stage_c(uc, par, pos)

    m_ref[...] = jnp.full(m_ref.shape, -jnp.inf, F32)
    l_ref[...] = jnp.zeros(l_ref.shape, F32)
    acc_ref[...] = jnp.zeros(acc_ref.shape, F32)

    first_full, last_full = 2, (n_items - 2) // 2
    last_round = (n_items + 3) // 2
    steady = range(first_full, last_full + 1) if last_full >= first_full else range(0)
    for i in range(last_round + 1):
        if i not in steady:
            if i == last_full + 1 and len(steady) > 0:
                def body(r, carry):
                    @pl.when(r % 2 == 0)
                    def _():
                        round_(r, 0, False)

                    @pl.when(r % 2 == 1)
                    def _():
                        round_(r, 1, False)
                    return carry
                lax.fori_loop(first_full, last_full + 1, body, 0)
            round_(i, i % 2, True)


def _mla_mixer(h, pad, w_dq, q_norm_g, w_uq, w_dkv, kv_norm_g, w_ukv, w_out, g, b):
    B, Tp, _ = h.shape
    cos, sin = _rope_tables(Tp, pad, MLA_ROPE)
    tm = _row_tile(Tp, 384)
    nqk = MLA_HEADS * MLA_QK
    nv = MLA_HEADS * MLA_V
    tab = pl.BlockSpec((tm, LANE), lambda b, i: (i, 0))
    q, k, v = pl.pallas_call(
        functools.partial(_mla_proj_kernel, pad=pad, tm=tm),
        grid=(B, Tp // tm),
        in_specs=[_rows(tm, D_MODEL), _whole(w_dq.shape), _whole((1, MLA_Q_RANK)), _whole(w_uq.shape),
                  _whole(w_dkv.shape), _whole((1, MLA_KV_RANK)), _whole(w_ukv.shape), tab, tab],
        out_specs=[_rows(tm, nqk), _rows(tm, nqk), _rows(tm, nv)],
        out_shape=[jax.ShapeDtypeStruct((B, Tp, nqk), BF16), jax.ShapeDtypeStruct((B, Tp, nqk), BF16),
                   jax.ShapeDtypeStruct((B, Tp, nv), BF16)],
        compiler_params=_cparams("parallel", "parallel"),
        name="mla_proj",
    )(h, w_dq, q_norm_g, w_uq, w_dkv, kv_norm_g, w_ukv, cos, sin)

    tq = _row_tile(Tp, MLA_Q_TILE_MAX)
    kc = _row_tile(Tp, MLA_KEY_CHUNK_MAX)
    o = pl.pallas_call(
        functools.partial(_mla_attn_kernel, tq=tq, kc=kc),
        grid=(B, MLA_HEADS),
        in_specs=[pl.BlockSpec((1, Tp, MLA_QK), lambda b, hd: (b, 0, hd)),
                  pl.BlockSpec((1, Tp, MLA_QK), lambda b, hd: (b, 0, hd)),
                  pl.BlockSpec((1, Tp, MLA_V), lambda b, hd: (b, 0, hd))],
        out_specs=pl.BlockSpec((1, Tp, MLA_V), lambda b, hd: (b, 0, hd)),
        out_shape=jax.ShapeDtypeStruct((B, Tp, nv), BF16),
        scratch_shapes=[pltpu.VMEM((2, 2, tq, kc), F32), pltpu.VMEM((2, 2, tq, kc), BF16),
                        pltpu.VMEM((2, 2, tq, LANE), F32), pltpu.VMEM((2, 2, tq, LANE), F32),
                        pltpu.VMEM((tq, LANE), F32), pltpu.VMEM((tq, LANE), F32),
                        pltpu.VMEM((tq, MLA_V), F32)],
        compiler_params=_cparams("parallel", "parallel"),
        name="mla_attn",
    )(q, k, v)
    return _proj_ln(o, w_out, h, g, b)


def _prepare_params(p):
    bf = lambda a: a.astype(BF16)
    row = lambda a: a.astype(F32).reshape(1, -1)
    out = dict(p)
    for name in ('a_w_in', 'a_w_out', 'b_w_grp', 'c_w_qkv', 'c_w_out', 'd_w_dq', 'd_w_out',
                 'ffn_w_gu', 'ffn_w_down'):
        out[name] = bf(p[name])
    n_d = p['d_w_uq'].shape[0]
    wuq = p['d_w_uq'].reshape(n_d, MLA_Q_RANK, MLA_HEADS, MLA_NOPE + MLA_ROPE)
    wuq = jnp.pad(wuq, ((0, 0), (0, 0), (0, 0), (0, MLA_QK - MLA_NOPE - MLA_ROPE)))
    out['d_w_uq'] = bf(wuq.reshape(n_d, MLA_Q_RANK, MLA_HEADS * MLA_QK))
    out['d_w_dkv'] = bf(jnp.pad(p['d_w_dkv'], ((0, 0), (0, 0), (0, LANE - MLA_ROPE))))
    wukv = p['d_w_ukv'].reshape(n_d, MLA_KV_RANK, MLA_HEADS, MLA_NOPE + MLA_V)
    out['d_w_ukv'] = bf(jnp.concatenate(
        [wukv[..., :MLA_NOPE].reshape(n_d, MLA_KV_RANK, MLA_HEADS * MLA_NOPE),
         wukv[..., MLA_NOPE:].reshape(n_d, MLA_KV_RANK, MLA_HEADS * MLA_V)], axis=-1))
    out['lb_all'] = jnp.cumsum(jax.nn.softmax(p['hg_lb_logits'].astype(F32), axis=0), axis=0)
    return out


def _trunk(x, p):
    B, S, _ = x.shape
    T = N_META + S
    Tp = -(-T // LANE) * LANE
    pad = Tp - T
    meta = jnp.broadcast_to(p['meta_tokens'].astype(x.dtype)[None], (B, N_META, D_MODEL))
    h = jnp.concatenate([jnp.zeros((B, pad, D_MODEL), x.dtype), meta, x], axis=1)
    row = lambda a: a.astype(F32).reshape(1, -1)
    for i in range(DEPTH):
        kind, j = i % N_MIXERS, i // N_MIXERS
        g0, b0 = row(p['ln_g'][i, 0]), row(p['ln_b'][i, 0])
        if kind == 0:
            h = _hgrn_mixer(h, pad, p['a_w_in'][j], p['a_w_out'][j], row(p['a_norm_g'][j]),
                            row(p['lb_all'][i]), g0, b0)
        elif kind == 1:
            h = _pool_mixer(h, pad, p['b_w_grp'][j], row(p['b_scale'][j]), g0, b0)
        elif kind == 2:
            h = _swa_mixer(h, pad, p['c_w_qkv'][j], p['c_w_out'][j], p['c_sink'][j].astype(F32), g0, b0)
        else:
            h = _mla_mixer(h, pad, p['d_w_dq'][j], row(p['d_q_norm_g'][j]), p['d_w_uq'][j],
                           p['d_w_dkv'][j], row(p['d_kv_norm_g'][j]), p['d_w_ukv'][j],
                           p['d_w_out'][j], g0, b0)
        h = _ffn(h, p['ffn_w_gu'][i], p['ffn_w_down'][i], row(p['ln_g'][i, 1]), row(p['ln_b'][i, 1]),
                 skip_rows=pad + N_META if i == DEPTH - 1 else 0)
    return h


def kernel(x_prompt, x_sample, meta_tokens, hg_lb_logits, a_w_in, a_w_out, a_norm_g, b_w_grp, b_scale,
           c_w_qkv, c_w_out, c_sink, d_w_dq, d_q_norm_g, d_w_uq, d_w_dkv, d_kv_norm_g, d_w_ukv, d_w_out,
           ffn_w_gu, ffn_w_down, ln_g, ln_b):
    params = _prepare_params({
        'meta_tokens': meta_tokens, 'hg_lb_logits': hg_lb_logits,
        'a_w_in': a_w_in, 'a_w_out': a_w_out, 'a_norm_g': a_norm_g,
        'b_w_grp': b_w_grp, 'b_scale': b_scale,
        'c_w_qkv': c_w_qkv, 'c_w_out': c_w_out, 'c_sink': c_sink,
        'd_w_dq': d_w_dq, 'd_q_norm_g': d_q_norm_g, 'd_w_uq': d_w_uq, 'd_w_dkv': d_w_dkv,
        'd_kv_norm_g': d_kv_norm_g, 'd_w_ukv': d_w_ukv, 'd_w_out': d_w_out,
        'ffn_w_gu': ffn_w_gu, 'ffn_w_down': ffn_w_down, 'ln_g': ln_g, 'ln_b': ln_b,
    })
    return (_trunk(x_prompt, params), _trunk(x_sample, params))
```

```python
import functools

import jax
import jax.numpy as jnp
from jax import lax
from jax.experimental import pallas as pl
from jax.experimental.pallas import tpu as pltpu

F32 = jnp.float32
BF16 = jnp.bfloat16

D_MODEL = 1024
DEPTH = 4
N_META = 16
N_MIXERS = 4
DN_ALPHA = (2 * DEPTH) ** 0.25
LN_EPS = 1e-5
RMS_EPS = 1e-6
ROPE_THETA = 10000.0

HG_HEADS = 8
HG_FDIM = 128
HG_IDIM = D_MODEL // HG_HEADS
POOL_WINDOWS = (2, 4, 8, 16)
POOL_GROUP = D_MODEL // len(POOL_WINDOWS)
POOL_HALO = 8
SW_HEADS = 8
SW_KV_HEADS = 2
SW_GROUPS = SW_HEADS // SW_KV_HEADS
SW_HEAD_DIM = D_MODEL // SW_HEADS
SW_WINDOW = 128
MLA_HEADS = 16
MLA_Q_RANK = 256
MLA_KV_RANK = 256
MLA_NOPE = 128
MLA_ROPE = 64
MLA_V = 128
MLA_QK = 256
MLA_Q_TILE_MAX = 640
MLA_KEY_CHUNK_MAX = 1536
D_FF = ((8 * D_MODEL + 3 * 256 - 1) // (3 * 256)) * 256

LANE = 128
HG_CHUNK = 128
VMEM_LIMIT_BYTES = 56 * 1024 * 1024
KEY_MASK_BIAS = -1e30
LOG2_E = 1.4426950408889634

NT_DIMS = (((1,), (1,)), ((), ()))


def _cparams(*sem):
    return pltpu.CompilerParams(dimension_semantics=sem, vmem_limit_bytes=VMEM_LIMIT_BYTES)


def _row_tile(tp, max_rows):
    best = LANE
    for t in range(LANE, max_rows + 1, LANE):
        if tp % t == 0:
            best = t
    return best


def _whole(shape):
    nd = len(shape)
    return pl.BlockSpec(shape, lambda *_: (0,) * nd, pipeline_mode=pl.Buffered(1))


def _rows(tm, width, col=0):
    return pl.BlockSpec((1, tm, width), lambda b, i: (b, i, col))


def _layer_norm(z, g, b):
    mu = jnp.mean(z, axis=-1, keepdims=True)
    zc = z - mu
    var = jnp.mean(zc * zc, axis=-1, keepdims=True)
    return zc * lax.rsqrt(var + LN_EPS) * g + b


def _rms_norm(x, g):
    return x * lax.rsqrt(jnp.mean(x * x, axis=-1, keepdims=True) + RMS_EPS) * g


def _sigmoid(x):
    return 1.0 / (1.0 + jnp.exp(-x))


def _silu_tanh(x):
    hx = 0.5 * x
    return hx * jnp.tanh(hx) + hx


def _silu(x):
    return x * _sigmoid(x)


def _bdot(a, b):
    return jnp.dot(a, b, preferred_element_type=F32)


def _matmul_kernel(x_ref, w_ref, o_ref):
    o_ref[0] = _bdot(x_ref[0].astype(BF16), w_ref[...])


def _flat(a):
    return a.reshape(1, a.shape[0] * a.shape[1], a.shape[2])


def _matmul(x, w, max_rows):
    B, Tp, K = x.shape
    N = w.shape[1]
    tm = _row_tile(B * Tp, max_rows)
    return pl.pallas_call(
        _matmul_kernel,
        grid=(1, B * Tp // tm),
        in_specs=[_rows(tm, K), _whole((K, N))],
        out_specs=_rows(tm, N),
        out_shape=jax.ShapeDtypeStruct((1, B * Tp, N), F32),
        compiler_params=_cparams("parallel", "parallel"),
        name="in_proj",
    )(_flat(x), w).reshape(B, Tp, N)


def _proj_ln_kernel(a_ref, w_ref, h_ref, g_ref, b_ref, o_ref):
    y = _bdot(a_ref[0], w_ref[...])
    o_ref[0] = _layer_norm(DN_ALPHA * h_ref[0] + y, g_ref[...], b_ref[...])


def _proj_ln(a, w, h, g, b, max_rows=640):
    B, Tp, K = a.shape
    tm = _row_tile(B * Tp, max_rows)
    return pl.pallas_call(
        _proj_ln_kernel,
        grid=(1, B * Tp // tm),
        in_specs=[_rows(tm, K), _whole((K, D_MODEL)), _rows(tm, D_MODEL),
                  _whole((1, D_MODEL)), _whole((1, D_MODEL))],
        out_specs=_rows(tm, D_MODEL),
        out_shape=jax.ShapeDtypeStruct((1, B * Tp, D_MODEL), F32),
        compiler_params=_cparams("parallel", "parallel"),
        name="out_proj_ln",
    )(_flat(a), w, _flat(h), g, b).reshape(B, Tp, D_MODEL)


FFN_CHUNKS = 11


def _ffn_kernel(x_ref, wgu_ref, wd_ref, g_ref, b_ref, o_ref):
    x = x_ref[...].reshape(x_ref.shape[-2:])
    xb = x.astype(BF16)
    ch = D_FF // FFN_CHUNKS
    y = None
    for c in range(FFN_CHUNKS):
        gate = _bdot(xb, wgu_ref[:, c * ch:(c + 1) * ch])
        up = _bdot(xb, wgu_ref[:, D_FF + c * ch:D_FF + (c + 1) * ch])
        act = (_silu(gate) * up).astype(BF16)
        part = _bdot(act, wd_ref[c * ch:(c + 1) * ch, :])
        y = part if y is None else y + part
    o_ref[0] = _layer_norm(DN_ALPHA * x + y, g_ref[...], b_ref[...])


def _ffn(h, wgu, wd, g, b, skip_rows=0):
    B, Tp, _ = h.shape
    rows_out = Tp - skip_rows
    if skip_rows:
        tm = _row_tile(rows_out, 640)
        h = h.reshape(B * Tp, D_MODEL)
        x_spec = pl.BlockSpec((pl.Element(tm), pl.Element(D_MODEL)),
                              lambda b, i: (pl.multiple_of(b * Tp + skip_rows + i * tm, LANE), 0))
        nb = B
    else:
        h = _flat(h)
        nb, rows_out = 1, B * Tp
        tm = _row_tile(rows_out, 640)
        x_spec = _rows(tm, D_MODEL)
    out = pl.pallas_call(
        _ffn_kernel,
        grid=(nb, rows_out // tm),
        in_specs=[x_spec, _whole((D_MODEL, 2 * D_FF)), _whole((D_FF, D_MODEL)),
                  _whole((1, D_MODEL)), _whole((1, D_MODEL))],
        out_specs=_rows(tm, D_MODEL),
        out_shape=jax.ShapeDtypeStruct((nb, rows_out, D_MODEL), F32),
        compiler_params=_cparams("parallel", "parallel"),
        name="ffn",
    )(h, wgu, wd, g, b)
    return out.reshape(B, -1, D_MODEL)


def _hgrn_direction_consts(reverse):
    C = HG_CHUNK
    row = lax.broadcasted_iota(jnp.int32, (C, C), 0)
    col = lax.broadcasted_iota(jnp.int32, (C, C), 1)
    rb, cb = row // 32, col // 32
    if reverse:
        cum = col >= row
        m1 = (row < 64) & (col >= 64)
        m2 = (rb // 2 == cb // 2) & (rb % 2 == 0) & (cb % 2 == 1)
        m3 = (rb == cb) & (col >= row)
        refs = (64, (32, 96), (16, 48, 80, 112), 0)
    else:
        cum = col <= row
        m1 = (row >= 64) & (col < 64)
        m2 = (rb // 2 == cb // 2) & (rb % 2 == 1) & (cb % 2 == 0)
        m3 = (rb == cb) & (col <= row)
        refs = (63, (31, 95), (15, 47, 79, 111), C - 1)
    cum = cum.astype(BF16)
    return jnp.concatenate([cum, cum], axis=1), (m1, m2, m3), refs


def _hgrn_head_chunk(q, v, a, lb, valid, st_ref, h, consts):
    cum2, (m1, m2, m3), (ref1, ref2, ref3, last) = consts
    t1 = lax.broadcasted_iota(jnp.int32, (HG_CHUNK, 1), 0)
    c0, c1 = 0.5 * (1.0 + lb), 0.5 * (1.0 - lb)
    ct = c1 * jnp.tanh(0.5 * a)
    f = c0 + ct
    k = jnp.where(valid, c1 - ct, 0.0)
    lf = jnp.where(valid, jnp.log(f) * LOG2_E, 0.0)
    lf0 = lf.astype(BF16)
    lf1 = (lf - lf0.astype(F32)).astype(BF16)
    b = _bdot(cum2, jnp.concatenate([lf0, lf1], axis=0))

    def brow(i):
        return b[i:i + 1, :]

    r1 = brow(ref1)
    r2 = jnp.where(t1 < 64, brow(ref2[0]), brow(ref2[1]))
    r3 = jnp.where(t1 < 32, brow(ref3[0]),
                   jnp.where(t1 < 64, brow(ref3[1]),
                             jnp.where(t1 < 96, brow(ref3[2]), brow(ref3[3]))))
    btot = brow(last)
    d1, d2, d3 = b - r1, b - r2, b - r3
    p1 = lax.dot_general((q * jnp.exp2(jnp.minimum(d1, 0.0))).astype(BF16),
                         (k * jnp.exp2(jnp.minimum(-d1, 0.0))).astype(BF16),
                         NT_DIMS, preferred_element_type=F32)
    p2 = lax.dot_general((q * jnp.exp2(jnp.minimum(d2, 0.0))).astype(BF16),
                         (k * jnp.exp2(jnp.minimum(-d2, 0.0))).astype(BF16),
                         NT_DIMS, preferred_element_type=F32)
    p3 = lax.dot_general((q * jnp.exp2(d3)).astype(BF16), (k * jnp.exp2(-d3)).astype(BF16),
                         NT_DIMS, preferred_element_type=F32)
    att = jnp.where(m1, p1, jnp.where(m2, p2, jnp.where(m3, p3, 0.0)))
    st = st_ref[h]
    o = _bdot(att.astype(BF16), v.astype(BF16))
    o = o + lax.dot_general((q * jnp.exp2(b)).astype(BF16), st.astype(BF16),
                            NT_DIMS, preferred_element_type=F32)
    ks = (k * jnp.exp2(btot - b)).astype(BF16)
    st_ref[h] = st * jnp.exp2(btot) + _bdot(v.T.astype(BF16), ks)
    return o


def _hgrn_scan_kernel(qf_ref, vf_ref, af_ref, qb_ref, vb_ref, ab_ref, lb_ref, of_ref, ob_ref,
                      stf_ref, stb_ref, *, n_chunks, pad):
    C = HG_CHUNK
    c = pl.program_id(1)

    @pl.when(c == 0)
    def _():
        stf_ref[...] = jnp.zeros_like(stf_ref)
        stb_ref[...] = jnp.zeros_like(stb_ref)

    t1 = lax.broadcasted_iota(jnp.int32, (C, 1), 0)
    consts_f = _hgrn_direction_consts(False)
    consts_b = _hgrn_direction_consts(True)
    valid_f = (c * C + t1) >= pad
    valid_b = ((n_chunks - 1 - c) * C + t1) >= pad
    for h in range(HG_HEADS):
        sl = slice(h * HG_FDIM, (h + 1) * HG_FDIM)
        lb = lb_ref[:, sl]
        of_ref[0, :, sl] = _hgrn_head_chunk(_silu_tanh(qf_ref[0, :, sl]), vf_ref[0, :, sl], af_ref[0, :, sl],
                                            lb, valid_f, stf_ref, h, consts_f)
        ob_ref[0, :, sl] = _hgrn_head_chunk(_silu_tanh(qb_ref[0, :, sl]), vb_ref[0, :, sl], ab_ref[0, :, sl],
                                            lb, valid_b, stb_ref, h, consts_b)


def _hgrn_scan(proj, lb, pad):
    B, Tp, _ = proj.shape
    C = HG_CHUNK
    n = Tp // C
    width = HG_HEADS * HG_FDIM

    def fwd(colblk):
        return pl.BlockSpec((1, C, width), lambda b, c: (b, c, colblk))

    def bwd(colblk):
        return pl.BlockSpec((1, C, width), lambda b, c: (b, n - 1 - c, colblk))

    state = pltpu.VMEM((HG_HEADS, HG_IDIM, HG_FDIM), F32)
    return pl.pallas_call(
        functools.partial(_hgrn_scan_kernel, n_chunks=n, pad=pad),
        grid=(B, n),
        in_specs=[fwd(0), fwd(1), fwd(2), bwd(0), bwd(1), bwd(3), _whole((1, width))],
        out_specs=[fwd(0), bwd(0)],
        out_shape=[jax.ShapeDtypeStruct((B, Tp, width), F32), jax.ShapeDtypeStruct((B, Tp, width), F32)],
        scratch_shapes=[state, state],
        compiler_params=_cparams("parallel", "arbitrary"),
        name="hgrn_scan",
    )(proj, proj, proj, proj, proj, proj, lb)


def _hgrn_out_kernel(of_ref, ob_ref, gate_ref, ng_ref, w_ref, h_ref, g_ref, b_ref, o_ref):
    o = of_ref[0] + ob_ref[0]
    parts = []
    for hd in range(HG_HEADS):
        sl = slice(hd * HG_IDIM, (hd + 1) * HG_IDIM)
        parts.append(_rms_norm(o[:, sl], ng_ref[...]))
    on = jnp.concatenate(parts, axis=1) * _silu(gate_ref[0])
    y = _bdot(on.astype(BF16), w_ref[...])
    o_ref[0] = _layer_norm(DN_ALPHA * h_ref[0] + y, g_ref[...], b_ref[...])


def _hgrn_out(o_f, o_b, proj, norm_g, w_out, h, g, b):
    B, Tp, _ = h.shape
    tm = _row_tile(B * Tp, 640)
    return pl.pallas_call(
        _hgrn_out_kernel,
        grid=(1, B * Tp // tm),
        in_specs=[_rows(tm, D_MODEL), _rows(tm, D_MODEL), _rows(tm, D_MODEL, col=4),
                  _whole((1, HG_IDIM)), _whole((D_MODEL, D_MODEL)), _rows(tm, D_MODEL),
                  _whole((1, D_MODEL)), _whole((1, D_MODEL))],
        out_specs=_rows(tm, D_MODEL),
        out_shape=jax.ShapeDtypeStruct((1, B * Tp, D_MODEL), F32),
        compiler_params=_cparams("parallel", "parallel"),
        name="hgrn_out",
    )(_flat(o_f), _flat(o_b), _flat(proj), norm_g, w_out, _flat(h), g, b).reshape(B, Tp, D_MODEL)


def _hgrn_mixer(h, pad, w_in, w_out, norm_g, lb, g, b):
    proj = _matmul(h, w_in, 640)
    o_f, o_b = _hgrn_scan(proj, lb, pad)
    return _hgrn_out(o_f, o_b, proj, norm_g, w_out, h, g, b)


def _pool_kernel(xp_ref, x_ref, xn_ref, w_ref, sc_ref, g_ref, b_ref, o_ref, xe_ref, *, tt, pad, t_len):
    i = pl.program_id(1)
    H = POOL_HALO
    x = x_ref[0]

    def keep(vals, first_row):
        r = first_row + lax.broadcasted_iota(jnp.int32, (vals.shape[0], 1), 0)
        return jnp.where((r >= pad) & (r < pad + t_len), vals, 0.0)

    xe_ref[0:H, :] = keep(xp_ref[0], i * tt - H)
    xe_ref[H:H + tt, :] = keep(x, i * tt)
    xe_ref[H + tt:H + tt + H, :] = keep(xn_ref[0], i * tt + tt)

    tok = i * tt - pad + lax.broadcasted_iota(jnp.int32, (tt, 1), 0)
    ys = []
    for gi, w in enumerate(POOL_WINDOWS):
        sl = slice(gi * POOL_GROUP, (gi + 1) * POOL_GROUP)
        acc = None
        for j in range(-(w // 2), w // 2):
            piece = xe_ref[H + j:H + j + tt, sl]
            acc = piece if acc is None else acc + piece
        lo = jnp.clip(tok - w // 2, 0, t_len)
        hi = jnp.clip(tok + w // 2, 0, t_len)
        inv_cnt = 1.0 / jnp.maximum(hi - lo, 1).astype(F32)
        p = acc * inv_cnt - x[:, sl]
        ys.append(_bdot(p.astype(BF16), w_ref[gi]))
    y = jnp.concatenate(ys, axis=1) * sc_ref[...]
    o_ref[0] = _layer_norm(DN_ALPHA * x + y, g_ref[...], b_ref[...])


def _pool_mixer(h, pad, w_grp, scale, g, b):
    B, Tp, _ = h.shape
    tt = _row_tile(Tp, 640)
    H = POOL_HALO
    per = tt // H
    last = Tp // H - 1
    return pl.pallas_call(
        functools.partial(_pool_kernel, tt=tt, pad=pad, t_len=Tp - pad),
        grid=(B, Tp // tt),
        in_specs=[pl.BlockSpec((1, H, D_MODEL), lambda b, i: (b, jnp.maximum(i * per - 1, 0), 0)),
                  _rows(tt, D_MODEL),
                  pl.BlockSpec((1, H, D_MODEL), lambda b, i: (b, jnp.minimum((i + 1) * per, last), 0)),
                  _whole((len(POOL_WINDOWS), POOL_GROUP, POOL_GROUP)), _whole((1, D_MODEL)),
                  _whole((1, D_MODEL)), _whole((1, D_MODEL))],
        out_specs=_rows(tt, D_MODEL),
        out_shape=jax.ShapeDtypeStruct((B, Tp, D_MODEL), F32),
        scratch_shapes=[pltpu.VMEM((tt + 2 * H, D_MODEL), F32)],
        compiler_params=_cparams("parallel", "parallel"),
        name="pool_mixer",
    )(h, h, h, w_grp, scale, g, b)


def _rope_tables(tp, pad, d):
    inv = 1.0 / (ROPE_THETA ** (jnp.arange(0, d, 2, dtype=F32) / d))
    pos = (jnp.arange(tp) - pad).astype(F32)
    ang = pos[:, None] * inv[None, :]
    cos, sin = jnp.cos(ang), jnp.sin(ang)
    reps = LANE // d
    return (jnp.tile(jnp.concatenate([cos, cos], axis=1), (1, reps)),
            jnp.tile(jnp.concatenate([-sin, sin], axis=1), (1, reps)))


def _swa_proj_kernel(x_ref, w_ref, cos_ref, sin_ref, q_ref, k_ref, v_ref):
    qkv = _bdot(x_ref[0].astype(BF16), w_ref[...])
    cos, sin = cos_ref[...], sin_ref[...]
    nq = SW_HEADS * SW_HEAD_DIM
    nk = SW_KV_HEADS * SW_HEAD_DIM

    def rope(xh):
        return xh * cos + pltpu.roll(xh, SW_HEAD_DIM // 2, 1) * sin

    for hd in range(SW_HEADS):
        sl = slice(hd * SW_HEAD_DIM, (hd + 1) * SW_HEAD_DIM)
        q_ref[0, :, sl] = rope(qkv[:, sl]).astype(BF16)
    for hd in range(SW_KV_HEADS):
        sl = slice(hd * SW_HEAD_DIM, (hd + 1) * SW_HEAD_DIM)
        k_ref[0, :, sl] = rope(qkv[:, nq + hd * SW_HEAD_DIM:nq + (hd + 1) * SW_HEAD_DIM]).astype(BF16)
    v_ref[0] = qkv[:, nq + nk:].astype(BF16)


def _swa_attn_kernel(sink_ref, q_ref, kp_ref, kc_ref, kn_ref, vp_ref, vc_ref, vn_ref,
                     w_ref, h_ref, g_ref, b_ref, o_ref, kcat_ref, vcat_ref, att_ref, s_ref, p_ref,
                     *, pad, tp, tt):
    i = pl.program_id(1)
    blk = SW_WINDOW
    kcat_ref[0:blk, :] = kp_ref[0]
    kcat_ref[blk:blk + tt, :] = kc_ref[0]
    kcat_ref[blk + tt:, :] = kn_ref[0]
    vcat_ref[0:blk, :] = vp_ref[0]
    vcat_ref[blk:blk + tt, :] = vc_ref[0]
    vcat_ref[blk + tt:, :] = vn_ref[0]
    scale = SW_HEAD_DIM ** -0.5
    dq = lax.broadcasted_iota(jnp.int32, (blk, 3 * blk), 0)
    dk = lax.broadcasted_iota(jnp.int32, (blk, 3 * blk), 1) - blk

    units = [(j, kv) for j in range(tt // blk) for kv in range(SW_KV_HEADS)]

    def heads_of(kv):
        return range(kv * SW_GROUPS, (kv + 1) * SW_GROUPS)

    for n, (j, kv) in enumerate(units):
        qs = jnp.concatenate([q_ref[0, j * blk:(j + 1) * blk, hd * SW_HEAD_DIM:(hd + 1) * SW_HEAD_DIM]
                              for hd in heads_of(kv)], axis=0)
        s_ref[n] = lax.dot_general(qs, kcat_ref[j * blk:(j + 3) * blk, kv * SW_HEAD_DIM:(kv + 1) * SW_HEAD_DIM],
                                   NT_DIMS, preferred_element_type=F32)
    c = scale * LOG2_E
    band = jnp.abs(dq - dk) <= SW_WINDOW
    ok_of_block = {}
    for n, (j, kv) in enumerate(units):
        if j not in ok_of_block:
            rk = i * tt + j * blk + dk[:1]
            ok = band & ((rk >= pad) & (rk < tp))
            ok_of_block[j] = jnp.concatenate([ok] * SW_GROUPS, axis=0)
        sk = jnp.concatenate([jnp.full((blk, 1), sink_ref[hd] * (1.0 / scale), F32) for hd in heads_of(kv)],
                             axis=0)
        s = jnp.where(ok_of_block[j], s_ref[n], -jnp.inf)
        m = jnp.maximum(jnp.max(s, axis=-1, keepdims=True), sk)
        p = jnp.exp2((s - m) * c)
        den = jnp.sum(p, axis=-1, keepdims=True) + jnp.exp2((sk - m) * c)
        p_ref[n] = (p * (1.0 / den)).astype(BF16)
    for n, (j, kv) in enumerate(units):
        att = _bdot(p_ref[n], vcat_ref[j * blk:(j + 3) * blk, kv * SW_HEAD_DIM:(kv + 1) * SW_HEAD_DIM])
        att = att.astype(BF16)
        for g, hd in enumerate(heads_of(kv)):
            att_ref[j * blk:(j + 1) * blk, hd * SW_HEAD_DIM:(hd + 1) * SW_HEAD_DIM] = att[g * blk:(g + 1) * blk]
    y = _bdot(att_ref[...], w_ref[...])
    o_ref[0] = _layer_norm(DN_ALPHA * h_ref[0] + y, g_ref[...], b_ref[...])


def _swa_mixer(h, pad, w_qkv, w_out, sink, g, b):
    B, Tp, _ = h.shape
    cos, sin = _rope_tables(Tp, pad, SW_HEAD_DIM)
    tm = _row_tile(Tp, 640)
    nq = SW_HEADS * SW_HEAD_DIM
    nk = SW_KV_HEADS * SW_HEAD_DIM
    tab = pl.BlockSpec((tm, LANE), lambda b, i: (i, 0))
    q, k, v = pl.pallas_call(
        _swa_proj_kernel,
        grid=(B, Tp // tm),
        in_specs=[_rows(tm, D_MODEL), _whole((D_MODEL, nq + 2 * nk)), tab, tab],
        out_specs=[_rows(tm, nq), _rows(tm, nk), _rows(tm, nk)],
        out_shape=[jax.ShapeDtypeStruct((B, Tp, nq), BF16), jax.ShapeDtypeStruct((B, Tp, nk), BF16),
                   jax.ShapeDtypeStruct((B, Tp, nk), BF16)],
        compiler_params=_cparams("parallel", "parallel"),
        name="swa_proj",
    )(h, w_qkv, cos, sin)

    blk = SW_WINDOW
    nb = Tp // blk
    tt = tm
    per = tt // blk
    prev = pl.BlockSpec((1, blk, nk), lambda b, i: (b, jnp.maximum(i * per - 1, 0), 0))
    cur = pl.BlockSpec((1, tt, nk), lambda b, i: (b, i, 0))
    nxt = pl.BlockSpec((1, blk, nk), lambda b, i: (b, jnp.minimum((i + 1) * per, nb - 1), 0))
    return pl.pallas_call(
        functools.partial(_swa_attn_kernel, pad=pad, tp=Tp, tt=tt),
        grid=(B, Tp // tt),
        in_specs=[pl.BlockSpec(memory_space=pltpu.SMEM),
                  _rows(tt, nq), prev, cur, nxt, prev, cur, nxt,
                  _whole((nq, D_MODEL)), _rows(tt, D_MODEL),
                  _whole((1, D_MODEL)), _whole((1, D_MODEL))],
        out_specs=_rows(tt, D_MODEL),
        out_shape=jax.ShapeDtypeStruct((B, Tp, D_MODEL), F32),
        scratch_shapes=[pltpu.VMEM((tt + 2 * blk, nk), BF16), pltpu.VMEM((tt + 2 * blk, nk), BF16),
                        pltpu.VMEM((tt, nq), BF16),
                        pltpu.VMEM((per * SW_KV_HEADS, SW_GROUPS * blk, 3 * blk), F32),
                        pltpu.VMEM((per * SW_KV_HEADS, SW_GROUPS * blk, 3 * blk), BF16)],
        compiler_params=_cparams("parallel", "parallel"),
        name="swa_attn",
    )(sink, q, k, k, k, v, v, v, w_out, h, g, b)


def _mla_proj_kernel(x_ref, wdq_ref, gq_ref, wuq_ref, wdkv_ref, gkv_ref, wukv_ref, cos_ref, sin_ref,
                     q_ref, k_ref, v_ref, *, pad, tm):
    i = pl.program_id(1)
    xb = x_ref[0].astype(BF16)
    cq = _rms_norm(_bdot(xb, wdq_ref[...]), gq_ref[...])
    q = _bdot(cq.astype(BF16), wuq_ref[...])
    ckv = _bdot(xb, wdkv_ref[...])
    c = _rms_norm(ckv[:, :MLA_KV_RANK], gkv_ref[...])
    kv = _bdot(c.astype(BF16), wukv_ref[...])
    cos, sin = cos_ref[...], sin_ref[...]
    lane = lax.broadcasted_iota(jnp.int32, (tm, LANE), 1)
    low = (lane % MLA_ROPE) < MLA_ROPE // 2
    spare = lane == MLA_ROPE

    def rope(xr):
        swapped = jnp.where(low, pltpu.roll(xr, LANE - MLA_ROPE // 2, 1), pltpu.roll(xr, MLA_ROPE // 2, 1))
        return xr * cos + swapped * sin

    filler = (i * tm + lax.broadcasted_iota(jnp.int32, (tm, 1), 0)) < pad
    kr = jnp.where(spare, jnp.where(filler, KEY_MASK_BIAS, 0.0), rope(ckv[:, MLA_KV_RANK:])).astype(BF16)
    for hd in range(MLA_HEADS):
        base = hd * MLA_QK
        q_ref[0, :, base:base + MLA_NOPE] = q[:, base:base + MLA_NOPE].astype(BF16)
        qr = jnp.where(spare, 1.0, rope(q[:, base + MLA_NOPE:base + MLA_QK]))
        q_ref[0, :, base + MLA_NOPE:base + MLA_QK] = qr.astype(BF16)
        k_ref[0, :, base:base + MLA_NOPE] = kv[:, hd * MLA_NOPE:(hd + 1) * MLA_NOPE].astype(BF16)
        k_ref[0, :, base + MLA_NOPE:base + MLA_QK] = kr
    v_ref[0] = kv[:, MLA_HEADS * MLA_NOPE:].astype(BF16)


def _mla_attn_kernel(q_ref, k_ref, v_ref, o_ref, s_ref, p_ref, alpha_ref, linv_ref, m_ref, l_ref, acc_ref,
                     *, tq, kc):
    c = (MLA_NOPE + MLA_ROPE) ** -0.5 * LOG2_E
    tp = k_ref.shape[1]
    nc = tp // kc
    n_items = (tp // tq) * nc
    n_sub = kc // LANE

    def tile_chunk(t):
        if isinstance(t, int):
            return divmod(t, nc)
        qi = lax.div(t, jnp.int32(nc))
        return qi, t - qi * nc

    def rows(i, size):
        return pl.ds(i * size, size) if isinstance(i, int) else pl.ds(pl.multiple_of(i * size, size), size)

    def stage_a(u, st, pos):
        qi, j = tile_chunk(u)
        s_ref[st, pos] = lax.dot_general(q_ref[0, rows(qi, tq), :], k_ref[0, rows(j, kc), :], NT_DIMS,
                                         preferred_element_type=F32)

    def stage_b(u, st, pos):
        _, j = tile_chunk(u)
        m = jnp.where(j == 0, -jnp.inf, m_ref[...])
        parts = [s_ref[st, pos, :, i * LANE:(i + 1) * LANE] for i in range(n_sub)]
        m_new = jnp.maximum(m, jnp.max(functools.reduce(jnp.maximum, parts), axis=-1, keepdims=True))
        alpha = jnp.exp2((m - m_new) * c)
        ps = [jnp.exp2((sp - m_new) * c) for sp in parts]
        l = alpha * l_ref[...] + functools.reduce(jnp.add, ps)
        l_ref[...] = l
        m_ref[...] = m_new
        alpha_ref[st, pos] = alpha
        linv_ref[st, pos] = jnp.broadcast_to(1.0 / jnp.sum(l, axis=-1, keepdims=True), l.shape)
        for i in range(n_sub):
            p_ref[st, pos, :, i * LANE:(i + 1) * LANE] = ps[i].astype(BF16)

    def stage_c(u, st, pos):
        qi, j = tile_chunk(u)
        acc = alpha_ref[st, pos] * acc_ref[...] + _bdot(p_ref[st, pos], v_ref[0, rows(j, kc), :])
        acc_ref[...] = acc
        o_ref[0, rows(qi, tq), :] = (acc * linv_ref[st, pos]).astype(BF16)

    def round_(i, par, static):
        for pos in (0, 1):
            ua, ub, uc = 2 * i + pos, 2 * i - 2 + pos, 2 * i - 4 + pos
            if not static or 0 <= ua < n_items:
                stage_a(ua, par, pos)
            if not static or 0 <= ub < n_items:
                stage_b(ub, 1 - par, pos)
            if not static or 0 <= uc < n_items:
                stage_c(uc, par, pos)

    m_ref[...] = jnp.full(m_ref.shape, -jnp.inf, F32)
    l_ref[...] = jnp.zeros(l_ref.shape, F32)
    acc_ref[...] = jnp.zeros(acc_ref.shape, F32)

    first_full, last_full = 2, (n_items - 2) // 2
    last_round = (n_items + 3) // 2
    steady = range(first_full, last_full + 1) if last_full >= first_full else range(0)
    for i in range(last_round + 1):
        if i not in steady:
            if i == last_full + 1 and len(steady) > 0:
                def body(r, carry):
                    @pl.when(r % 2 == 0)
                    def _():
                        round_(r, 0, False)

                    @pl.when(r % 2 == 1)
                    def _():
                        round_(r, 1, False)
                    return carry
                lax.fori_loop(first_full, last_full + 1, body, 0)
            round_(i, i % 2, True)


def _mla_mixer(h, pad, w_dq, q_norm_g, w_uq, w_dkv, kv_norm_g, w_ukv, w_out, g, b):
    B, Tp, _ = h.shape
    cos, sin = _rope_tables(Tp, pad, MLA_ROPE)
    tm = _row_tile(Tp, 384)
    nqk = MLA_HEADS * MLA_QK
    nv = MLA_HEADS * MLA_V
    tab = pl.BlockSpec((tm, LANE), lambda b, i: (i, 0))
    q, k, v = pl.pallas_call(
        functools.partial(_mla_proj_kernel, pad=pad, tm=tm),
        grid=(B, Tp // tm),
        in_specs=[_rows(tm, D_MODEL), _whole(w_dq.shape), _whole((1, MLA_Q_RANK)), _whole(w_uq.shape),
                  _whole(w_dkv.shape), _whole((1, MLA_KV_RANK)), _whole(w_ukv.shape), tab, tab],
        out_specs=[_rows(tm, nqk), _rows(tm, nqk), _rows(tm, nv)],
        out_shape=[jax.ShapeDtypeStruct((B, Tp, nqk), BF16), jax.ShapeDtypeStruct((B, Tp, nqk), BF16),
                   jax.ShapeDtypeStruct((B, Tp, nv), BF16)],
        compiler_params=_cparams("parallel", "parallel"),
        name="mla_proj",
    )(h, w_dq, q_norm_g, w_uq, w_dkv, kv_norm_g, w_ukv, cos, sin)

    tq = _row_tile(Tp, MLA_Q_TILE_MAX)
    kc = _row_tile(Tp, MLA_KEY_CHUNK_MAX)
    o = pl.pallas_call(
        functools.partial(_mla_attn_kernel, tq=tq, kc=kc),
        grid=(B, MLA_HEADS),
        in_specs=[pl.BlockSpec((1, Tp, MLA_QK), lambda b, hd: (b, 0, hd)),
                  pl.BlockSpec((1, Tp, MLA_QK), lambda b, hd: (b, 0, hd)),
                  pl.BlockSpec((1, Tp, MLA_V), lambda b, hd: (b, 0, hd))],
        out_specs=pl.BlockSpec((1, Tp, MLA_V), lambda b, hd: (b, 0, hd)),
        out_shape=jax.ShapeDtypeStruct((B, Tp, nv), BF16),
        scratch_shapes=[pltpu.VMEM((2, 2, tq, kc), F32), pltpu.VMEM((2, 2, tq, kc), BF16),
                        pltpu.VMEM((2, 2, tq, LANE), F32), pltpu.VMEM((2, 2, tq, LANE), F32),
                        pltpu.VMEM((tq, LANE), F32), pltpu.VMEM((tq, LANE), F32),
                        pltpu.VMEM((tq, MLA_V), F32)],
        compiler_params=_cparams("parallel", "parallel"),
        name="mla_attn",
    )(q, k, v)
    return _proj_ln(o, w_out, h, g, b)


def _prepare_params(p):
    bf = lambda a: a.astype(BF16)
    row = lambda a: a.astype(F32).reshape(1, -1)
    out = dict(p)
    for name in ('a_w_in', 'a_w_out', 'b_w_grp', 'c_w_qkv', 'c_w_out', 'd_w_dq', 'd_w_out',
                 'ffn_w_gu', 'ffn_w_down'):
        out[name] = bf(p[name])
    n_d = p['d_w_uq'].shape[0]
    wuq = p['d_w_uq'].reshape(n_d, MLA_Q_RANK, MLA_HEADS, MLA_NOPE + MLA_ROPE)
    wuq = jnp.pad(wuq, ((0, 0), (0, 0), (0, 0), (0, MLA_QK - MLA_NOPE - MLA_ROPE)))
    out['d_w_uq'] = bf(wuq.reshape(n_d, MLA_Q_RANK, MLA_HEADS * MLA_QK))
    out['d_w_dkv'] = bf(jnp.pad(p['d_w_dkv'], ((0, 0), (0, 0), (0, LANE - MLA_ROPE))))
    wukv = p['d_w_ukv'].reshape(n_d, MLA_KV_RANK, MLA_HEADS, MLA_NOPE + MLA_V)
    out['d_w_ukv'] = bf(jnp.concatenate(
        [wukv[..., :MLA_NOPE].reshape(n_d, MLA_KV_RANK, MLA_HEADS * MLA_NOPE),
         wukv[..., MLA_NOPE:].reshape(n_d, MLA_KV_RANK, MLA_HEADS * MLA_V)], axis=-1))
    out['lb_all'] = jnp.cumsum(jax.nn.softmax(p['hg_lb_logits'].astype(F32), axis=0), axis=0)
    return out


def _trunk(x, p):
    B, S, _ = x.shape
    T = N_META + S
    Tp = -(-T // LANE) * LANE
    pad = Tp - T
    meta = jnp.broadcast_to(p['meta_tokens'].astype(x.dtype)[None], (B, N_META, D_MODEL))
    h = jnp.concatenate([jnp.zeros((B, pad, D_MODEL), x.dtype), meta, x], axis=1)
    row = lambda a: a.astype(F32).reshape(1, -1)
    for i in range(DEPTH):
        kind, j = i % N_MIXERS, i // N_MIXERS
        g0, b0 = row(p['ln_g'][i, 0]), row(p['ln_b'][i, 0])
        if kind == 0:
            h = _hgrn_mixer(h, pad, p['a_w_in'][j], p['a_w_out'][j], row(p['a_norm_g'][j]),
                            row(p['lb_all'][i]), g0, b0)
        elif kind == 1:
            h = _pool_mixer(h, pad, p['b_w_grp'][j], row(p['b_scale'][j]), g0, b0)
        elif kind == 2:
            h = _swa_mixer(h, pad, p['c_w_qkv'][j], p['c_w_out'][j], p['c_sink'][j].astype(F32), g0, b0)
        else:
            h = _mla_mixer(h, pad, p['d_w_dq'][j], row(p['d_q_norm_g'][j]), p['d_w_uq'][j],
                           p['d_w_dkv'][j], row(p['d_kv_norm_g'][j]), p['d_w_ukv'][j],
                           p['d_w_out'][j], g0, b0)
        h = _ffn(h, p['ffn_w_gu'][i], p['ffn_w_down'][i], row(p['ln_g'][i, 1]), row(p['ln_b'][i, 1]),
                 skip_rows=pad + N_META if i == DEPTH - 1 else 0)
    return h


def kernel(x_prompt, x_sample, meta_tokens, hg_lb_logits, a_w_in, a_w_out, a_norm_g, b_w_grp, b_scale,
           c_w_qkv, c_w_out, c_sink, d_w_dq, d_q_norm_g, d_w_uq, d_w_dkv, d_kv_norm_g, d_w_ukv, d_w_out,
           ffn_w_gu, ffn_w_down, ln_g, ln_b):
    params = _prepare_params({
        'meta_tokens': meta_tokens, 'hg_lb_logits': hg_lb_logits,
        'a_w_in': a_w_in, 'a_w_out': a_w_out, 'a_norm_g': a_norm_g,
        'b_w_grp': b_w_grp, 'b_scale': b_scale,
        'c_w_qkv': c_w_qkv, 'c_w_out': c_w_out, 'c_sink': c_sink,
        'd_w_dq': d_w_dq, 'd_q_norm_g': d_q_norm_g, 'd_w_uq': d_w_uq, 'd_w_dkv': d_w_dkv,
        'd_kv_norm_g': d_kv_norm_g, 'd_w_ukv': d_w_ukv, 'd_w_out': d_w_out,
        'ffn_w_gu': ffn_w_gu, 'ffn_w_down': ffn_w_down, 'ln_g': ln_g, 'ln_b': ln_b,
    })
    return (_trunk(x_prompt, params), _trunk(x_sample, params))
```

```python
import functools

import jax
import jax.numpy as jnp
from jax import lax
from jax.experimental import pallas as pl
from jax.experimental.pallas import tpu as pltpu

F32 = jnp.float32
BF16 = jnp.bfloat16

D_MODEL = 1024
DEPTH = 4
N_META = 16
N_MIXERS = 4
DN_ALPHA = (2 * DEPTH) ** 0.25
LN_EPS = 1e-5
RMS_EPS = 1e-6
ROPE_THETA = 10000.0

HG_HEADS = 8
HG_FDIM = 128
HG_IDIM = D_MODEL // HG_HEADS
POOL_WINDOWS = (2, 4, 8, 16)
POOL_GROUP = D_MODEL // len(POOL_WINDOWS)
POOL_HALO = 8
SW_HEADS = 8
SW_KV_HEADS = 2
SW_GROUPS = SW_HEADS // SW_KV_HEADS
SW_HEAD_DIM = D_MODEL // SW_HEADS
SW_WINDOW = 128
MLA_HEADS = 16
MLA_Q_RANK = 256
MLA_KV_RANK = 256
MLA_NOPE = 128
MLA_ROPE = 64
MLA_V = 128
MLA_QK = 256
MLA_Q_TILE_MAX = 640
MLA_KEY_CHUNK_MAX = 1792
D_FF = ((8 * D_MODEL + 3 * 256 - 1) // (3 * 256)) * 256

LANE = 128
HG_CHUNK = 128
VMEM_LIMIT_BYTES = 56 * 1024 * 1024
KEY_MASK_BIAS = -1e30
LOG2_E = 1.4426950408889634

NT_DIMS = (((1,), (1,)), ((), ()))


def _cparams(*sem):
    return pltpu.CompilerParams(dimension_semantics=sem, vmem_limit_bytes=VMEM_LIMIT_BYTES)


def _row_tile(tp, max_rows):
    best = LANE
    for t in range(LANE, max_rows + 1, LANE):
        if tp % t == 0:
            best = t
    return best


def _whole(shape):
    nd = len(shape)
    return pl.BlockSpec(shape, lambda *_: (0,) * nd, pipeline_mode=pl.Buffered(1))


def _rows(tm, width, col=0):
    return pl.BlockSpec((1, tm, width), lambda b, i: (b, i, col))


def _layer_norm(z, g, b):
    mu = jnp.mean(z, axis=-1, keepdims=True)
    zc = z - mu
    var = jnp.mean(zc * zc, axis=-1, keepdims=True)
    return zc * lax.rsqrt(var + LN_EPS) * g + b


def _rms_norm(x, g):
    return x * lax.rsqrt(jnp.mean(x * x, axis=-1, keepdims=True) + RMS_EPS) * g


def _sigmoid(x):
    return 1.0 / (1.0 + jnp.exp(-x))


def _silu_tanh(x):
    hx = 0.5 * x
    return hx * jnp.tanh(hx) + hx


def _silu(x):
    return x * _sigmoid(x)


def _bdot(a, b):
    return jnp.dot(a, b, preferred_element_type=F32)


def _matmul_kernel(x_ref, w_ref, o_ref):
    o_ref[0] = _bdot(x_ref[0].astype(BF16), w_ref[...])


def _flat(a):
    return a.reshape(1, a.shape[0] * a.shape[1], a.shape[2])


def _matmul(x, w, max_rows):
    B, Tp, K = x.shape
    N = w.shape[1]
    tm = _row_tile(B * Tp, max_rows)
    return pl.pallas_call(
        _matmul_kernel,
        grid=(1, B * Tp // tm),
        in_specs=[_rows(tm, K), _whole((K, N))],
        out_specs=_rows(tm, N),
        out_shape=jax.ShapeDtypeStruct((1, B * Tp, N), F32),
        compiler_params=_cparams("parallel", "parallel"),
        name="in_proj",
    )(_flat(x), w).reshape(B, Tp, N)


def _proj_ln_kernel(a_ref, w_ref, h_ref, g_ref, b_ref, o_ref):
    y = _bdot(a_ref[0], w_ref[...])
    o_ref[0] = _layer_norm(DN_ALPHA * h_ref[0] + y, g_ref[...], b_ref[...])


def _proj_ln(a, w, h, g, b, max_rows=640):
    B, Tp, K = a.shape
    tm = _row_tile(B * Tp, max_rows)
    return pl.pallas_call(
        _proj_ln_kernel,
        grid=(1, B * Tp // tm),
        in_specs=[_rows(tm, K), _whole((K, D_MODEL)), _rows(tm, D_MODEL),
                  _whole((1, D_MODEL)), _whole((1, D_MODEL))],
        out_specs=_rows(tm, D_MODEL),
        out_shape=jax.ShapeDtypeStruct((1, B * Tp, D_MODEL), F32),
        compiler_params=_cparams("parallel", "parallel"),
        name="out_proj_ln",
    )(_flat(a), w, _flat(h), g, b).reshape(B, Tp, D_MODEL)


FFN_CHUNKS = 11


def _ffn_kernel(x_ref, wgu_ref, wd_ref, g_ref, b_ref, o_ref):
    x = x_ref[...].reshape(x_ref.shape[-2:])
    xb = x.astype(BF16)
    ch = D_FF // FFN_CHUNKS
    y = None
    for c in range(FFN_CHUNKS):
        gate = _bdot(xb, wgu_ref[:, c * ch:(c + 1) * ch])
        up = _bdot(xb, wgu_ref[:, D_FF + c * ch:D_FF + (c + 1) * ch])
        act = (_silu(gate) * up).astype(BF16)
        part = _bdot(act, wd_ref[c * ch:(c + 1) * ch, :])
        y = part if y is None else y + part
    o_ref[0] = _layer_norm(DN_ALPHA * x + y, g_ref[...], b_ref[...])


def _ffn(h, wgu, wd, g, b, skip_rows=0):
    B, Tp, _ = h.shape
    rows_out = Tp - skip_rows
    if skip_rows:
        tm = _row_tile(rows_out, 640)
        h = h.reshape(B * Tp, D_MODEL)
        x_spec = pl.BlockSpec((pl.Element(tm), pl.Element(D_MODEL)),
                              lambda b, i: (pl.multiple_of(b * Tp + skip_rows + i * tm, LANE), 0))
        nb = B
    else:
        h = _flat(h)
        nb, rows_out = 1, B * Tp
        tm = _row_tile(rows_out, 640)
        x_spec = _rows(tm, D_MODEL)
    out = pl.pallas_call(
        _ffn_kernel,
        grid=(nb, rows_out // tm),
        in_specs=[x_spec, _whole((D_MODEL, 2 * D_FF)), _whole((D_FF, D_MODEL)),
                  _whole((1, D_MODEL)), _whole((1, D_MODEL))],
        out_specs=_rows(tm, D_MODEL),
        out_shape=jax.ShapeDtypeStruct((nb, rows_out, D_MODEL), F32),
        compiler_params=_cparams("parallel", "parallel"),
        name="ffn",
    )(h, wgu, wd, g, b)
    return out.reshape(B, -1, D_MODEL)


def _hgrn_direction_consts(reverse):
    C = HG_CHUNK
    row = lax.broadcasted_iota(jnp.int32, (C, C), 0)
    col = lax.broadcasted_iota(jnp.int32, (C, C), 1)
    rb, cb = row // 32, col // 32
    if reverse:
        cum = col >= row
        m1 = (row < 64) & (col >= 64)
        m2 = (rb // 2 == cb // 2) & (rb % 2 == 0) & (cb % 2 == 1)
        m3 = (rb == cb) & (col >= row)
        refs = (64, (32, 96), (16, 48, 80, 112), 0)
    else:
        cum = col <= row
        m1 = (row >= 64) & (col < 64)
        m2 = (rb // 2 == cb // 2) & (rb % 2 == 1) & (cb % 2 == 0)
        m3 = (rb == cb) & (col <= row)
        refs = (63, (31, 95), (15, 47, 79, 111), C - 1)
    cum = cum.astype(BF16)
    return jnp.concatenate([cum, cum], axis=1), (m1, m2, m3), refs


def _hgrn_head_chunk(q, v, a, lb, valid, st_ref, h, consts):
    cum2, (m1, m2, m3), (ref1, ref2, ref3, last) = consts
    t1 = lax.broadcasted_iota(jnp.int32, (HG_CHUNK, 1), 0)
    c0, c1 = 0.5 * (1.0 + lb), 0.5 * (1.0 - lb)
    ct = c1 * jnp.tanh(0.5 * a)
    f = c0 + ct
    k = jnp.where(valid, c1 - ct, 0.0)
    lf = jnp.where(valid, jnp.log(f) * LOG2_E, 0.0)
    lf0 = lf.astype(BF16)
    lf1 = (lf - lf0.astype(F32)).astype(BF16)
    b = _bdot(cum2, jnp.concatenate([lf0, lf1], axis=0))

    def brow(i):
        return b[i:i + 1, :]

    r1 = brow(ref1)
    r2 = jnp.where(t1 < 64, brow(ref2[0]), brow(ref2[1]))
    r3 = jnp.where(t1 < 32, brow(ref3[0]),
                   jnp.where(t1 < 64, brow(ref3[1]),
                             jnp.where(t1 < 96, brow(ref3[2]), brow(ref3[3]))))
    btot = brow(last)
    d1, d2, d3 = b - r1, b - r2, b - r3
    p1 = lax.dot_general((q * jnp.exp2(jnp.minimum(d1, 0.0))).astype(BF16),
                         (k * jnp.exp2(jnp.minimum(-d1, 0.0))).astype(BF16),
                         NT_DIMS, preferred_element_type=F32)
    p2 = lax.dot_general((q * jnp.exp2(jnp.minimum(d2, 0.0))).astype(BF16),
                         (k * jnp.exp2(jnp.minimum(-d2, 0.0))).astype(BF16),
                         NT_DIMS, preferred_element_type=F32)
    p3 = lax.dot_general((q * jnp.exp2(d3)).astype(BF16), (k * jnp.exp2(-d3)).astype(BF16),
                         NT_DIMS, preferred_element_type=F32)
    att = jnp.where(m1, p1, jnp.where(m2, p2, jnp.where(m3, p3, 0.0)))
    st = st_ref[h]
    o = _bdot(att.astype(BF16), v.astype(BF16))
    o = o + lax.dot_general((q * jnp.exp2(b)).astype(BF16), st.astype(BF16),
                            NT_DIMS, preferred_element_type=F32)
    ks = (k * jnp.exp2(btot - b)).astype(BF16)
    st_ref[h] = st * jnp.exp2(btot) + _bdot(v.T.astype(BF16), ks)
    return o


def _hgrn_scan_kernel(qf_ref, vf_ref, af_ref, qb_ref, vb_ref, ab_ref, lb_ref, of_ref, ob_ref,
                      stf_ref, stb_ref, *, n_chunks, pad):
    C = HG_CHUNK
    c = pl.program_id(1)

    @pl.when(c == 0)
    def _():
        stf_ref[...] = jnp.zeros_like(stf_ref)
        stb_ref[...] = jnp.zeros_like(stb_ref)

    t1 = lax.broadcasted_iota(jnp.int32, (C, 1), 0)
    consts_f = _hgrn_direction_consts(False)
    consts_b = _hgrn_direction_consts(True)
    valid_f = (c * C + t1) >= pad
    valid_b = ((n_chunks - 1 - c) * C + t1) >= pad
    for h in range(HG_HEADS):
        sl = slice(h * HG_FDIM, (h + 1) * HG_FDIM)
        lb = lb_ref[:, sl]
        of_ref[0, :, sl] = _hgrn_head_chunk(_silu_tanh(qf_ref[0, :, sl]), vf_ref[0, :, sl], af_ref[0, :, sl],
                                            lb, valid_f, stf_ref, h, consts_f)
        ob_ref[0, :, sl] = _hgrn_head_chunk(_silu_tanh(qb_ref[0, :, sl]), vb_ref[0, :, sl], ab_ref[0, :, sl],
                                            lb, valid_b, stb_ref, h, consts_b)


def _hgrn_scan(proj, lb, pad):
    B, Tp, _ = proj.shape
    C = HG_CHUNK
    n = Tp // C
    width = HG_HEADS * HG_FDIM

    def fwd(colblk):
        return pl.BlockSpec((1, C, width), lambda b, c: (b, c, colblk))

    def bwd(colblk):
        return pl.BlockSpec((1, C, width), lambda b, c: (b, n - 1 - c, colblk))

    state = pltpu.VMEM((HG_HEADS, HG_IDIM, HG_FDIM), F32)
    return pl.pallas_call(
        functools.partial(_hgrn_scan_kernel, n_chunks=n, pad=pad),
        grid=(B, n),
        in_specs=[fwd(0), fwd(1), fwd(2), bwd(0), bwd(1), bwd(3), _whole((1, width))],
        out_specs=[fwd(0), bwd(0)],
        out_shape=[jax.ShapeDtypeStruct((B, Tp, width), F32), jax.ShapeDtypeStruct((B, Tp, width), F32)],
        scratch_shapes=[state, state],
        compiler_params=_cparams("parallel", "arbitrary"),
        name="hgrn_scan",
    )(proj, proj, proj, proj, proj, proj, lb)


def _hgrn_out_kernel(of_ref, ob_ref, gate_ref, ng_ref, w_ref, h_ref, g_ref, b_ref, o_ref):
    o = of_ref[0] + ob_ref[0]
    parts = []
    for hd in range(HG_HEADS):
        sl = slice(hd * HG_IDIM, (hd + 1) * HG_IDIM)
        parts.append(_rms_norm(o[:, sl], ng_ref[...]))
    on = jnp.concatenate(parts, axis=1) * _silu(gate_ref[0])
    y = _bdot(on.astype(BF16), w_ref[...])
    o_ref[0] = _layer_norm(DN_ALPHA * h_ref[0] + y, g_ref[...], b_ref[...])


def _hgrn_out(o_f, o_b, proj, norm_g, w_out, h, g, b):
    B, Tp, _ = h.shape
    tm = _row_tile(B * Tp, 640)
    return pl.pallas_call(
        _hgrn_out_kernel,
        grid=(1, B * Tp // tm),
        in_specs=[_rows(tm, D_MODEL), _rows(tm, D_MODEL), _rows(tm, D_MODEL, col=4),
                  _whole((1, HG_IDIM)), _whole((D_MODEL, D_MODEL)), _rows(tm, D_MODEL),
                  _whole((1, D_MODEL)), _whole((1, D_MODEL))],
        out_specs=_rows(tm, D_MODEL),
        out_shape=jax.ShapeDtypeStruct((1, B * Tp, D_MODEL), F32),
        compiler_params=_cparams("parallel", "parallel"),
        name="hgrn_out",
    )(_flat(o_f), _flat(o_b), _flat(proj), norm_g, w_out, _flat(h), g, b).reshape(B, Tp, D_MODEL)


def _hgrn_mixer(h, pad, w_in, w_out, norm_g, lb, g, b):
    proj = _matmul(h, w_in, 640)
    o_f, o_b = _hgrn_scan(proj, lb, pad)
    return _hgrn_out(o_f, o_b, proj, norm_g, w_out, h, g, b)


def _pool_kernel(xp_ref, x_ref, xn_ref, w_ref, sc_ref, g_ref, b_ref, o_ref, xe_ref, *, tt, pad, t_len):
    i = pl.program_id(1)
    H = POOL_HALO
    x = x_ref[0]

    def keep(vals, first_row):
        r = first_row + lax.broadcasted_iota(jnp.int32, (vals.shape[0], 1), 0)
        return jnp.where((r >= pad) & (r < pad + t_len), vals, 0.0)

    xe_ref[0:H, :] = keep(xp_ref[0], i * tt - H)
    xe_ref[H:H + tt, :] = keep(x, i * tt)
    xe_ref[H + tt:H + tt + H, :] = keep(xn_ref[0], i * tt + tt)

    tok = i * tt - pad + lax.broadcasted_iota(jnp.int32, (tt, 1), 0)
    ys = []
    for gi, w in enumerate(POOL_WINDOWS):
        sl = slice(gi * POOL_GROUP, (gi + 1) * POOL_GROUP)
        acc = None
        for j in range(-(w // 2), w // 2):
            piece = xe_ref[H + j:H + j + tt, sl]
            acc = piece if acc is None else acc + piece
        lo = jnp.clip(tok - w // 2, 0, t_len)
        hi = jnp.clip(tok + w // 2, 0, t_len)
        inv_cnt = 1.0 / jnp.maximum(hi - lo, 1).astype(F32)
        p = acc * inv_cnt - x[:, sl]
        ys.append(_bdot(p.astype(BF16), w_ref[gi]))
    y = jnp.concatenate(ys, axis=1) * sc_ref[...]
    o_ref[0] = _layer_norm(DN_ALPHA * x + y, g_ref[...], b_ref[...])


def _pool_mixer(h, pad, w_grp, scale, g, b):
    B, Tp, _ = h.shape
    tt = _row_tile(Tp, 640)
    H = POOL_HALO
    per = tt // H
    last = Tp // H - 1
    return pl.pallas_call(
        functools.partial(_pool_kernel, tt=tt, pad=pad, t_len=Tp - pad),
        grid=(B, Tp // tt),
        in_specs=[pl.BlockSpec((1, H, D_MODEL), lambda b, i: (b, jnp.maximum(i * per - 1, 0), 0)),
                  _rows(tt, D_MODEL),
                  pl.BlockSpec((1, H, D_MODEL), lambda b, i: (b, jnp.minimum((i + 1) * per, last), 0)),
                  _whole((len(POOL_WINDOWS), POOL_GROUP, POOL_GROUP)), _whole((1, D_MODEL)),
                  _whole((1, D_MODEL)), _whole((1, D_MODEL))],
        out_specs=_rows(tt, D_MODEL),
        out_shape=jax.ShapeDtypeStruct((B, Tp, D_MODEL), F32),
        scratch_shapes=[pltpu.VMEM((tt + 2 * H, D_MODEL), F32)],
        compiler_params=_cparams("parallel", "parallel"),
        name="pool_mixer",
    )(h, h, h, w_grp, scale, g, b)


def _rope_tables(tp, pad, d):
    inv = 1.0 / (ROPE_THETA ** (jnp.arange(0, d, 2, dtype=F32) / d))
    pos = (jnp.arange(tp) - pad).astype(F32)
    ang = pos[:, None] * inv[None, :]
    cos, sin = jnp.cos(ang), jnp.sin(ang)
    reps = LANE // d
    return (jnp.tile(jnp.concatenate([cos, cos], axis=1), (1, reps)),
            jnp.tile(jnp.concatenate([-sin, sin], axis=1), (1, reps)))


def _swa_proj_kernel(x_ref, w_ref, cos_ref, sin_ref, q_ref, k_ref, v_ref):
    qkv = _bdot(x_ref[0].astype(BF16), w_ref[...])
    cos, sin = cos_ref[...], sin_ref[...]
    nq = SW_HEADS * SW_HEAD_DIM
    nk = SW_KV_HEADS * SW_HEAD_DIM

    def rope(xh):
        return xh * cos + pltpu.roll(xh, SW_HEAD_DIM // 2, 1) * sin

    for hd in range(SW_HEADS):
        sl = slice(hd * SW_HEAD_DIM, (hd + 1) * SW_HEAD_DIM)
        q_ref[0, :, sl] = rope(qkv[:, sl]).astype(BF16)
    for hd in range(SW_KV_HEADS):
        sl = slice(hd * SW_HEAD_DIM, (hd + 1) * SW_HEAD_DIM)
        k_ref[0, :, sl] = rope(qkv[:, nq + hd * SW_HEAD_DIM:nq + (hd + 1) * SW_HEAD_DIM]).astype(BF16)
    v_ref[0] = qkv[:, nq + nk:].astype(BF16)


def _swa_attn_kernel(sink_ref, q_ref, kp_ref, kc_ref, kn_ref, vp_ref, vc_ref, vn_ref,
                     w_ref, h_ref, g_ref, b_ref, o_ref, kcat_ref, vcat_ref, att_ref, s_ref, p_ref,
                     *, pad, tp, tt):
    i = pl.program_id(1)
    blk = SW_WINDOW
    kcat_ref[0:blk, :] = kp_ref[0]
    kcat_ref[blk:blk + tt, :] = kc_ref[0]
    kcat_ref[blk + tt:, :] = kn_ref[0]
    vcat_ref[0:blk, :] = vp_ref[0]
    vcat_ref[blk:blk + tt, :] = vc_ref[0]
    vcat_ref[blk + tt:, :] = vn_ref[0]
    scale = SW_HEAD_DIM ** -0.5
    dq = lax.broadcasted_iota(jnp.int32, (blk, 3 * blk), 0)
    dk = lax.broadcasted_iota(jnp.int32, (blk, 3 * blk), 1) - blk

    units = [(j, kv) for j in range(tt // blk) for kv in range(SW_KV_HEADS)]

    def heads_of(kv):
        return range(kv * SW_GROUPS, (kv + 1) * SW_GROUPS)

    for n, (j, kv) in enumerate(units):
        qs = jnp.concatenate([q_ref[0, j * blk:(j + 1) * blk, hd * SW_HEAD_DIM:(hd + 1) * SW_HEAD_DIM]
                              for hd in heads_of(kv)], axis=0)
        s_ref[n] = lax.dot_general(qs, kcat_ref[j * blk:(j + 3) * blk, kv * SW_HEAD_DIM:(kv + 1) * SW_HEAD_DIM],
                                   NT_DIMS, preferred_element_type=F32)
    c = scale * LOG2_E
    band = jnp.abs(dq - dk) <= SW_WINDOW
    ok_of_block = {}
    for n, (j, kv) in enumerate(units):
        if j not in ok_of_block:
            rk = i * tt + j * blk + dk[:1]
            ok = band & ((rk >= pad) & (rk < tp))
            ok_of_block[j] = jnp.concatenate([ok] * SW_GROUPS, axis=0)
        sk = jnp.concatenate([jnp.full((blk, 1), sink_ref[hd] * (1.0 / scale), F32) for hd in heads_of(kv)],
                             axis=0)
        s = jnp.where(ok_of_block[j], s_ref[n], -jnp.inf)
        m = jnp.maximum(jnp.max(s, axis=-1, keepdims=True), sk)
        p = jnp.exp2((s - m) * c)
        den = jnp.sum(p, axis=-1, keepdims=True) + jnp.exp2((sk - m) * c)
        p_ref[n] = (p * (1.0 / den)).astype(BF16)
    for n, (j, kv) in enumerate(units):
        att = _bdot(p_ref[n], vcat_ref[j * blk:(j + 3) * blk, kv * SW_HEAD_DIM:(kv + 1) * SW_HEAD_DIM])
        att = att.astype(BF16)
        for g, hd in enumerate(heads_of(kv)):
            att_ref[j * blk:(j + 1) * blk, hd * SW_HEAD_DIM:(hd + 1) * SW_HEAD_DIM] = att[g * blk:(g + 1) * blk]
    y = _bdot(att_ref[...], w_ref[...])
    o_ref[0] = _layer_norm(DN_ALPHA * h_ref[0] + y, g_ref[...], b_ref[...])


def _swa_mixer(h, pad, w_qkv, w_out, sink, g, b):
    B, Tp, _ = h.shape
    cos, sin = _rope_tables(Tp, pad, SW_HEAD_DIM)
    tm = _row_tile(Tp, 640)
    nq = SW_HEADS * SW_HEAD_DIM
    nk = SW_KV_HEADS * SW_HEAD_DIM
    tab = pl.BlockSpec((tm, LANE), lambda b, i: (i, 0))
    q, k, v = pl.pallas_call(
        _swa_proj_kernel,
        grid=(B, Tp // tm),
        in_specs=[_rows(tm, D_MODEL), _whole((D_MODEL, nq + 2 * nk)), tab, tab],
        out_specs=[_rows(tm, nq), _rows(tm, nk), _rows(tm, nk)],
        out_shape=[jax.ShapeDtypeStruct((B, Tp, nq), BF16), jax.ShapeDtypeStruct((B, Tp, nk), BF16),
                   jax.ShapeDtypeStruct((B, Tp, nk), BF16)],
        compiler_params=_cparams("parallel", "parallel"),
        name="swa_proj",
    )(h, w_qkv, cos, sin)

    blk = SW_WINDOW
    nb = Tp // blk
    tt = tm
    per = tt // blk
    prev = pl.BlockSpec((1, blk, nk), lambda b, i: (b, jnp.maximum(i * per - 1, 0), 0))
    cur = pl.BlockSpec((1, tt, nk), lambda b, i: (b, i, 0))
    nxt = pl.BlockSpec((1, blk, nk), lambda b, i: (b, jnp.minimum((i + 1) * per, nb - 1), 0))
    return pl.pallas_call(
        functools.partial(_swa_attn_kernel, pad=pad, tp=Tp, tt=tt),
        grid=(B, Tp // tt),
        in_specs=[pl.BlockSpec(memory_space=pltpu.SMEM),
                  _rows(tt, nq), prev, cur, nxt, prev, cur, nxt,
                  _whole((nq, D_MODEL)), _rows(tt, D_MODEL),
                  _whole((1, D_MODEL)), _whole((1, D_MODEL))],
        out_specs=_rows(tt, D_MODEL),
        out_shape=jax.ShapeDtypeStruct((B, Tp, D_MODEL), F32),
        scratch_shapes=[pltpu.VMEM((tt + 2 * blk, nk), BF16), pltpu.VMEM((tt + 2 * blk, nk), BF16),
                        pltpu.VMEM((tt, nq), BF16),
                        pltpu.VMEM((per * SW_KV_HEADS, SW_GROUPS * blk, 3 * blk), F32),
                        pltpu.VMEM((per * SW_KV_HEADS, SW_GROUPS * blk, 3 * blk), BF16)],
        compiler_params=_cparams("parallel", "parallel"),
        name="swa_attn",
    )(sink, q, k, k, k, v, v, v, w_out, h, g, b)


def _mla_proj_kernel(x_ref, wdq_ref, gq_ref, wuq_ref, wdkv_ref, gkv_ref, wukv_ref, cos_ref, sin_ref,
                     q_ref, k_ref, v_ref, *, pad, tm):
    i = pl.program_id(1)
    xb = x_ref[0].astype(BF16)
    cq = _rms_norm(_bdot(xb, wdq_ref[...]), gq_ref[...])
    q = _bdot(cq.astype(BF16), wuq_ref[...])
    ckv = _bdot(xb, wdkv_ref[...])
    c = _rms_norm(ckv[:, :MLA_KV_RANK], gkv_ref[...])
    kv = _bdot(c.astype(BF16), wukv_ref[...])
    cos, sin = cos_ref[...], sin_ref[...]
    lane = lax.broadcasted_iota(jnp.int32, (tm, LANE), 1)
    low = (lane % MLA_ROPE) < MLA_ROPE // 2
    spare = lane == MLA_ROPE

    def rope(xr):
        swapped = jnp.where(low, pltpu.roll(xr, LANE - MLA_ROPE // 2, 1), pltpu.roll(xr, MLA_ROPE // 2, 1))
        return xr * cos + swapped * sin

    filler = (i * tm + lax.broadcasted_iota(jnp.int32, (tm, 1), 0)) < pad
    kr = jnp.where(spare, jnp.where(filler, KEY_MASK_BIAS, 0.0), rope(ckv[:, MLA_KV_RANK:])).astype(BF16)
    for hd in range(MLA_HEADS):
        base = hd * MLA_QK
        q_ref[0, :, base:base + MLA_NOPE] = q[:, base:base + MLA_NOPE].astype(BF16)
        qr = jnp.where(spare, 1.0, rope(q[:, base + MLA_NOPE:base + MLA_QK]))
        q_ref[0, :, base + MLA_NOPE:base + MLA_QK] = qr.astype(BF16)
        k_ref[0, :, base:base + MLA_NOPE] = kv[:, hd * MLA_NOPE:(hd + 1) * MLA_NOPE].astype(BF16)
        k_ref[0, :, base + MLA_NOPE:base + MLA_QK] = kr
    v_ref[0] = kv[:, MLA_HEADS * MLA_NOPE:].astype(BF16)


def _mla_attn_kernel(q_ref, k_ref, v_ref, o_ref, s_ref, p_ref, alpha_ref, linv_ref, m_ref, l_ref, acc_ref,
                     *, tq, kc):
    c = (MLA_NOPE + MLA_ROPE) ** -0.5 * LOG2_E
    tp = k_ref.shape[1]
    nc = tp // kc
    n_items = (tp // tq) * nc
    n_sub = kc // LANE

    def tile_chunk(t):
        if isinstance(t, int):
            return divmod(t, nc)
        qi = lax.div(t, jnp.int32(nc))
        return qi, t - qi * nc

    def rows(i, size):
        return pl.ds(i * size, size) if isinstance(i, int) else pl.ds(pl.multiple_of(i * size, size), size)

    def stage_a(u, st, pos):
        qi, j = tile_chunk(u)
        s_ref[st, pos] = lax.dot_general(q_ref[0, rows(qi, tq), :], k_ref[0, rows(j, kc), :], NT_DIMS,
                                         preferred_element_type=F32)

    def stage_b(u, st, pos):
        _, j = tile_chunk(u)
        m = jnp.where(j == 0, -jnp.inf, m_ref[...])
        parts = [s_ref[st, pos, :, i * LANE:(i + 1) * LANE] for i in range(n_sub)]
        m_new = jnp.maximum(m, jnp.max(functools.reduce(jnp.maximum, parts), axis=-1, keepdims=True))
        alpha = jnp.exp2((m - m_new) * c)
        ps = [jnp.exp2((sp - m_new) * c) for sp in parts]
        l = alpha * l_ref[...] + functools.reduce(jnp.add, ps)
        l_ref[...] = l
        m_ref[...] = m_new
        alpha_ref[st, pos] = alpha
        linv_ref[st, pos] = jnp.broadcast_to(1.0 / jnp.sum(l, axis=-1, keepdims=True), l.shape)
        for i in range(n_sub):
            p_ref[st, pos, :, i * LANE:(i + 1) * LANE] = ps[i].astype(BF16)

    def stage_c(u, st, pos):
        qi, j = tile_chunk(u)
        acc = alpha_ref[st, pos] * acc_ref[...] + _bdot(p_ref[st, pos], v_ref[0, rows(j, kc), :])
        acc_ref[...] = acc
        o_ref[0, rows(qi, tq), :] = (acc * linv_ref[st, pos]).astype(BF16)

    def round_(i, par, static):
        for pos in (0, 1):
            ua, ub, uc = 2 * i + pos, 2 * i - 2 + pos, 2 * i - 4 + pos
            if not static or 0 <= ua < n_items:
                stage_a(ua, par, pos)
            if not static or 0 <= ub < n_items:
                stage_b(ub, 1 - par, pos)
            if not static or 0 <= uc < n_items:
                stage_c(uc, par, pos)

    m_ref[...] = jnp.full(m_ref.shape, -jnp.inf, F32)
    l_ref[...] = jnp.zeros(l_ref.shape, F32)
    acc_ref[...] = jnp.zeros(acc_ref.shape, F32)

    first_full, last_full = 2, (n_items - 2) // 2
    last_round = (n_items + 3) // 2
    steady = range(first_full, last_full + 1) if last_full >= first_full else range(0)
    for i in range(last_round + 1):
        if i not in steady:
            if i == last_full + 1 and len(steady) > 0:
                def body(r, carry):
                    @pl.when(r % 2 == 0)
                    def _():
                        round_(r, 0, False)

                    @pl.when(r % 2 == 1)
                    def _():
                        round_(r, 1, False)
                    return carry
                lax.fori_loop(first_full, last_full + 1, body, 0)
            round_(i, i % 2, True)


def _mla_mixer(h, pad, w_dq, q_norm_g, w_uq, w_dkv, kv_norm_g, w_ukv, w_out, g, b):
    B, Tp, _ = h.shape
    cos, sin = _rope_tables(Tp, pad, MLA_ROPE)
    tm = _row_tile(Tp, 384)
    nqk = MLA_HEADS * MLA_QK
    nv = MLA_HEADS * MLA_V
    tab = pl.BlockSpec((tm, LANE), lambda b, i: (i, 0))
    q, k, v = pl.pallas_call(
        functools.partial(_mla_proj_kernel, pad=pad, tm=tm),
        grid=(B, Tp // tm),
        in_specs=[_rows(tm, D_MODEL), _whole(w_dq.shape), _whole((1, MLA_Q_RANK)), _whole(w_uq.shape),
                  _whole(w_dkv.shape), _whole((1, MLA_KV_RANK)), _whole(w_ukv.shape), tab, tab],
        out_specs=[_rows(tm, nqk), _rows(tm, nqk), _rows(tm, nv)],
        out_shape=[jax.ShapeDtypeStruct((B, Tp, nqk), BF16), jax.ShapeDtypeStruct((B, Tp, nqk), BF16),
                   jax.ShapeDtypeStruct((B, Tp, nv), BF16)],
        compiler_params=_cparams("parallel", "parallel"),
        name="mla_proj",
    )(h, w_dq, q_norm_g, w_uq, w_dkv, kv_norm_g, w_ukv, cos, sin)

    tq = _row_tile(Tp, MLA_Q_TILE_MAX)
    kc = _row_tile(Tp, MLA_KEY_CHUNK_MAX)
    o = pl.pallas_call(
        functools.partial(_mla_attn_kernel, tq=tq, kc=kc),
        grid=(B, MLA_HEADS),
        in_specs=[pl.BlockSpec((1, Tp, MLA_QK), lambda b, hd: (b, 0, hd), pipeline_mode=pl.Buffered(1)),
                  pl.BlockSpec((1, Tp, MLA_QK), lambda b, hd: (b, 0, hd), pipeline_mode=pl.Buffered(1)),
                  pl.BlockSpec((1, Tp, MLA_V), lambda b, hd: (b, 0, hd), pipeline_mode=pl.Buffered(1))],
        out_specs=pl.BlockSpec((1, Tp, MLA_V), lambda b, hd: (b, 0, hd)),
        out_shape=jax.ShapeDtypeStruct((B, Tp, nv), BF16),
        scratch_shapes=[pltpu.VMEM((2, 2, tq, kc), F32), pltpu.VMEM((2, 2, tq, kc), BF16),
                        pltpu.VMEM((2, 2, tq, LANE), F32), pltpu.VMEM((2, 2, tq, LANE), F32),
                        pltpu.VMEM((tq, LANE), F32), pltpu.VMEM((tq, LANE), F32),
                        pltpu.VMEM((tq, MLA_V), F32)],
        compiler_params=_cparams("parallel", "parallel"),
        name="mla_attn",
    )(q, k, v)
    return _proj_ln(o, w_out, h, g, b)


def _prepare_params(p):
    bf = lambda a: a.astype(BF16)
    row = lambda a: a.astype(F32).reshape(1, -1)
    out = dict(p)
    for name in ('a_w_in', 'a_w_out', 'b_w_grp', 'c_w_qkv', 'c_w_out', 'd_w_dq', 'd_w_out',
                 'ffn_w_gu', 'ffn_w_down'):
        out[name] = bf(p[name])
    n_d = p['d_w_uq'].shape[0]
    wuq = p['d_w_uq'].reshape(n_d, MLA_Q_RANK, MLA_HEADS, MLA_NOPE + MLA_ROPE)
    wuq = jnp.pad(wuq, ((0, 0), (0, 0), (0, 0), (0, MLA_QK - MLA_NOPE - MLA_ROPE)))
    out['d_w_uq'] = bf(wuq.reshape(n_d, MLA_Q_RANK, MLA_HEADS * MLA_QK))
    out['d_w_dkv'] = bf(jnp.pad(p['d_w_dkv'], ((0, 0), (0, 0), (0, LANE - MLA_ROPE))))
    wukv = p['d_w_ukv'].reshape(n_d, MLA_KV_RANK, MLA_HEADS, MLA_NOPE + MLA_V)
    out['d_w_ukv'] = bf(jnp.concatenate(
        [wukv[..., :MLA_NOPE].reshape(n_d, MLA_KV_RANK, MLA_HEADS * MLA_NOPE),
         wukv[..., MLA_NOPE:].reshape(n_d, MLA_KV_RANK, MLA_HEADS * MLA_V)], axis=-1))
    out['lb_all'] = jnp.cumsum(jax.nn.softmax(p['hg_lb_logits'].astype(F32), axis=0), axis=0)
    return out


def _trunk(x, p):
    B, S, _ = x.shape
    T = N_META + S
    Tp = -(-T // LANE) * LANE
    pad = Tp - T
    meta = jnp.broadcast_to(p['meta_tokens'].astype(x.dtype)[None], (B, N_META, D_MODEL))
    h = jnp.concatenate([jnp.zeros((B, pad, D_MODEL), x.dtype), meta, x], axis=1)
    row = lambda a: a.astype(F32).reshape(1, -1)
    for i in range(DEPTH):
        kind, j = i % N_MIXERS, i // N_MIXERS
        g0, b0 = row(p['ln_g'][i, 0]), row(p['ln_b'][i, 0])
        if kind == 0:
            h = _hgrn_mixer(h, pad, p['a_w_in'][j], p['a_w_out'][j], row(p['a_norm_g'][j]),
                            row(p['lb_all'][i]), g0, b0)
        elif kind == 1:
            h = _pool_mixer(h, pad, p['b_w_grp'][j], row(p['b_scale'][j]), g0, b0)
        elif kind == 2:
            h = _swa_mixer(h, pad, p['c_w_qkv'][j], p['c_w_out'][j], p['c_sink'][j].astype(F32), g0, b0)
        else:
            h = _mla_mixer(h, pad, p['d_w_dq'][j], row(p['d_q_norm_g'][j]), p['d_w_uq'][j],
                           p['d_w_dkv'][j], row(p['d_kv_norm_g'][j]), p['d_w_ukv'][j],
                           p['d_w_out'][j], g0, b0)
        h = _ffn(h, p['ffn_w_gu'][i], p['ffn_w_down'][i], row(p['ln_g'][i, 1]), row(p['ln_b'][i, 1]),
                 skip_rows=pad + N_META if i == DEPTH - 1 else 0)
    return h


def kernel(x_prompt, x_sample, meta_tokens, hg_lb_logits, a_w_in, a_w_out, a_norm_g, b_w_grp, b_scale,
           c_w_qkv, c_w_out, c_sink, d_w_dq, d_q_norm_g, d_w_uq, d_w_dkv, d_kv_norm_g, d_w_ukv, d_w_out,
           ffn_w_gu, ffn_w_down, ln_g, ln_b):
    params = _prepare_params({
        'meta_tokens': meta_tokens, 'hg_lb_logits': hg_lb_logits,
        'a_w_in': a_w_in, 'a_w_out': a_w_out, 'a_norm_g': a_norm_g,
        'b_w_grp': b_w_grp, 'b_scale': b_scale,
        'c_w_qkv': c_w_qkv, 'c_w_out': c_w_out, 'c_sink': c_sink,
        'd_w_dq': d_w_dq, 'd_q_norm_g': d_q_norm_g, 'd_w_uq': d_w_uq, 'd_w_dkv': d_w_dkv,
        'd_kv_norm_g': d_kv_norm_g, 'd_w_ukv': d_w_ukv, 'd_w_out': d_w_out,
        'ffn_w_gu': ffn_w_gu, 'ffn_w_down': ffn_w_down, 'ln_g': ln_g, 'ln_b': ln_b,
    })
    return (_trunk(x_prompt, params), _trunk(x_sample, params))
```
